```python
import math
import jax, jax.numpy as jnp
from jax import lax
import numpy as np

D_MODEL = 2048
BATCH = 2
SEQ = 16384
DEPTH = 1

MEM_LEN = 256
D_ATTN = D_MODEL // 2
D_POOL = D_MODEL - D_ATTN
DIFF_HEAD_DIM = 128
N_DIFF_HEADS = D_ATTN // (2 * DIFF_HEAD_DIM)
D_IN = 3 * D_ATTN + D_POOL
POOL_WINDOWS = (2, 4, 8, 16)
N_POOL_GROUPS = len(POOL_WINDOWS)
POOL_GROUP_DIM = D_POOL // N_POOL_GROUPS
N_XATTN_HEADS = 4
XATTN_HEAD_DIM = D_MODEL // N_XATTN_HEADS
D_FF = -(-8 * D_MODEL // (3 * 256)) * 256
Q_BLOCK = 128
EPS = 1e-6
NEG_INF = -1e30

kernel_name = "hymba_diffattn_pool_hybrid"


def _lambda_init(layer_idx):
    return 0.8 - 0.6 * math.exp(-0.3 * layer_idx)


def _rmsnorm(x, g):
    x32 = x.astype(jnp.float32)
    y = x32 * lax.rsqrt(jnp.mean(x32 * x32, axis=-1, keepdims=True) + EPS)
    return (y * g.astype(jnp.float32)).astype(x.dtype)


def _diff_attention(q, k, v, lam):
    B, S, H, _, d = q.shape
    nb = S // Q_BLOCK
    qh = q.transpose(0, 2, 3, 1, 4)
    kh = k.transpose(0, 2, 3, 1, 4)
    vh = v.transpose(0, 2, 1, 3)
    qb = qh.reshape(B, H, 2, nb, Q_BLOCK, d).transpose(3, 0, 1, 2, 4, 5)
    kpos = jnp.arange(S)
    scale = d ** -0.5

    def block(args):
        i, qi = args
        qpos = i * Q_BLOCK + jnp.arange(Q_BLOCK)
        s = jnp.einsum('bhmqd,bhmkd->bhmqk', qi, kh).astype(jnp.float32) * scale
        s = jnp.where(kpos[None, :] <= qpos[:, None], s, NEG_INF)
        p = jax.nn.softmax(s, axis=-1)
        wgt = p[:, :, 0] - lam * p[:, :, 1]
        return jnp.einsum('bhqk,bhkd->bhqd', wgt.astype(vh.dtype), vh)

    o = lax.map(block, (jnp.arange(nb), qb))
    return o.transpose(1, 0, 3, 2, 4).reshape(B, S, H, 2 * d)


def _multiscale_pool(u, w, scale):
    B, S, C = u.shape
    u32 = u.astype(jnp.float32)
    cs = jnp.concatenate([jnp.zeros((B, 1, C), jnp.float32), jnp.cumsum(u32, axis=1)], axis=1)
    csg = cs.reshape(B, S + 1, N_POOL_GROUPS, POOL_GROUP_DIM)
    ug = u32.reshape(B, S, N_POOL_GROUPS, POOL_GROUP_DIM)
    t = jnp.arange(S)
    outs = []
    for g, wl in enumerate(POOL_WINDOWS):
        upper = csg[:, 1:, g]
        lower = jnp.concatenate([jnp.zeros((B, wl - 1, POOL_GROUP_DIM), jnp.float32),
                                 csg[:, :S - wl + 1, g]], axis=1)
        cnt = jnp.minimum(t + 1, wl).astype(jnp.float32)[None, :, None]
        outs.append((upper - lower) / cnt - ug[:, :, g])
    pooled = jnp.stack(outs, axis=2).astype(u.dtype)
    mixed = jnp.einsum('bsgc,gcd->bsgd', pooled, w)
    return mixed.reshape(B, S, C) * scale


def _cross_attention(h, m, wq, wkv, wo):
    B, S, _ = h.shape
    M = m.shape[1]
    q = (h @ wq).reshape(B, S, N_XATTN_HEADS, XATTN_HEAD_DIM)
    kv = (m @ wkv).reshape(B, M, 2, N_XATTN_HEADS, XATTN_HEAD_DIM)
    k, v = kv[:, :, 0], kv[:, :, 1]
    s = jnp.einsum('bshd,bmhd->bhsm', q, k).astype(jnp.float32) * (XATTN_HEAD_DIM ** -0.5)
    p = jax.nn.softmax(s, axis=-1).astype(v.dtype)
    o = jnp.einsum('bhsm,bmhd->bshd', p, v).reshape(B, S, D_MODEL)
    return o @ wo


def _swiglu(h, w_gate, w_up, w_down):
    return (jax.nn.silu(h @ w_gate) * (h @ w_up)) @ w_down


def setup_inputs(seed: int = 0) -> dict:
    key = jax.random.key(seed)
    ks = jax.random.split(key, 24)

    def nrm(k, shape, s):
        return jax.random.normal(k, shape, jnp.float32) * s

    def gain(k, shape):
        return 1.0 + 0.02 * jax.random.normal(k, shape, jnp.float32)

    return {
        "x": nrm(ks[0], (BATCH, SEQ, D_MODEL), 1.0),
        "mem": nrm(ks[1], (BATCH, MEM_LEN, D_MODEL), 1.0),
        "norm_mix": gain(ks[2], (DEPTH, D_MODEL)),
        "w_in": nrm(ks[3], (DEPTH, D_MODEL, D_IN), D_MODEL ** -0.5),
        "lambda_q1": nrm(ks[4], (DEPTH, DIFF_HEAD_DIM), 0.1),
        "lambda_k1": nrm(ks[5], (DEPTH, DIFF_HEAD_DIM), 0.1),
        "lambda_q2": nrm(ks[6], (DEPTH, DIFF_HEAD_DIM), 0.1),
        "lambda_k2": nrm(ks[7], (DEPTH, DIFF_HEAD_DIM), 0.1),
        "subln": gain(ks[8], (DEPTH, 2 * DIFF_HEAD_DIM)),
        "pool_w": nrm(ks[9], (DEPTH, N_POOL_GROUPS, POOL_GROUP_DIM, POOL_GROUP_DIM), POOL_GROUP_DIM ** -0.5),
        "pool_scale": gain(ks[10], (DEPTH, D_POOL)),
        "w_o": nrm(ks[11], (DEPTH, D_MODEL, D_MODEL), D_MODEL ** -0.5),
        "norm_xattn": gain(ks[12], (DEPTH, D_MODEL)),
        "norm_mem": gain(ks[13], (DEPTH, D_MODEL)),
        "wq_x": nrm(ks[14], (DEPTH, D_MODEL, D_MODEL), D_MODEL ** -0.5),
        "wkv_x": nrm(ks[15], (DEPTH, D_MODEL, 2 * D_MODEL), D_MODEL ** -0.5),
        "wo_x": nrm(ks[16], (DEPTH, D_MODEL, D_MODEL), D_MODEL ** -0.5),
        "norm_ffn": gain(ks[17], (DEPTH, D_MODEL)),
        "w_gate": nrm(ks[18], (DEPTH, D_MODEL, D_FF), D_MODEL ** -0.5),
        "w_up": nrm(ks[19], (DEPTH, D_MODEL, D_FF), D_MODEL ** -0.5),
        "w_down": nrm(ks[20], (DEPTH, D_FF, D_MODEL), D_FF ** -0.5),
        "norm_final": gain(ks[21], (D_MODEL,)),
    }


def reference(x, mem, norm_mix, w_in, lambda_q1, lambda_k1, lambda_q2, lambda_k2,
              subln, pool_w, pool_scale, w_o, norm_xattn, norm_mem, wq_x, wkv_x,
              wo_x, norm_ffn, w_gate, w_up, w_down, norm_final):
    B, S, _ = x.shape
    for l in range(DEPTH):
        lam_init = _lambda_init(l)
        h = _rmsnorm(x, norm_mix[l])
        proj = h @ w_in[l]
        q = proj[..., :D_ATTN].reshape(B, S, N_DIFF_HEADS, 2, DIFF_HEAD_DIM)
        k = proj[..., D_ATTN:2 * D_ATTN].reshape(B, S, N_DIFF_HEADS, 2, DIFF_HEAD_DIM)
        v = proj[..., 2 * D_ATTN:3 * D_ATTN].reshape(B, S, N_DIFF_HEADS, 2 * DIFF_HEAD_DIM)
        u = proj[..., 3 * D_ATTN:]
        lam = (jnp.exp(jnp.sum(lambda_q1[l].astype(jnp.float32) * lambda_k1[l].astype(jnp.float32)))
               - jnp.exp(jnp.sum(lambda_q2[l].astype(jnp.float32) * lambda_k2[l].astype(jnp.float32)))
               + lam_init)
        o_attn = _diff_attention(q, k, v, lam)
        o_attn = (_rmsnorm(o_attn, subln[l]) * (1.0 - lam_init)).reshape(B, S, D_ATTN)
        o_pool = _multiscale_pool(u, pool_w[l], pool_scale[l])
        x = x + jnp.concatenate([o_attn, o_pool], axis=-1) @ w_o[l]
        hx = _rmsnorm(x, norm_xattn[l])
        m = _rmsnorm(mem, norm_mem[l])
        x = x + _cross_attention(hx, m, wq_x[l], wkv_x[l], wo_x[l])
        hf = _rmsnorm(x, norm_ffn[l])
        x = x + _swiglu(hf, w_gate[l], w_up[l], w_down[l])
    return _rmsnorm(x, norm_final)
```

```python
import functools
import math

import jax
import jax.numpy as jnp
from jax import lax
from jax.experimental import pallas as pl
from jax.experimental.pallas import tpu as pltpu

MEM_LEN = 256
DIFF_HEAD_DIM = 128
POOL_WINDOWS = (2, 4, 8, 16)
N_XATTN_HEADS = 4
EPS = 1e-6
NEG_INF = -1e30
LOG2E = math.log2(math.e)

V7X_VMEM_BYTES = 64 * 1024 * 1024
V7X_LANES = 128
V7X_SUBLANES = 8
VMEM_LIMIT_BYTES = V7X_VMEM_BYTES - 8 * 1024 * 1024

MM_TM = 1024
MM_TN = 1024
ATT_TQ = 1024
ATT_CK = 512
POOL_TM = 512
POOL_HALO = 32
XATT_TM = 512
FFN_TM = 512
FFN_TF = 512

BF16 = jnp.bfloat16
F32 = jnp.float32


def _lambda_init(layer_idx):
    return 0.8 - 0.6 * math.exp(-0.3 * layer_idx)


def _rmsnorm_f32(x, g):
    y = x * lax.rsqrt(jnp.mean(x * x, axis=-1, keepdims=True) + EPS)
    return y * g


def _lane_tile(x, reps):
    return x if reps == 1 else jnp.concatenate([x] * reps, axis=1)


def _params(n_axes):
    return pltpu.CompilerParams(
        dimension_semantics=("arbitrary",) * n_axes,
        vmem_limit_bytes=VMEM_LIMIT_BYTES,
    )


def _norm_matmul_kernel(x_ref, g_ref, w_ref, cs_ref, o_ref, h_ref):
    @pl.when(pl.program_id(1) == 0)
    def _():
        h_ref[...] = _rmsnorm_f32(x_ref[...], g_ref[...]).astype(h_ref.dtype)

    acc = jnp.dot(h_ref[...], w_ref[...], preferred_element_type=F32)
    o_ref[...] = (acc * cs_ref[...]).astype(o_ref.dtype)


def _norm_matmul(x, g, w, col_scale, *, tm, tn, name):
    m, k = x.shape
    n = w.shape[1]
    assert m % tm == 0 and n % tn == 0
    return pl.pallas_call(
        _norm_matmul_kernel,
        grid=(m // tm, n // tn),
        in_specs=[
            pl.BlockSpec((tm, k), lambda i, j: (i, 0)),
            pl.BlockSpec((1, k), lambda i, j: (0, 0)),
            pl.BlockSpec((k, tn), lambda i, j: (0, j)),
            pl.BlockSpec((1, tn), lambda i, j: (0, j)),
        ],
        out_specs=pl.BlockSpec((tm, tn), lambda i, j: (i, j)),
        out_shape=jax.ShapeDtypeStruct((m, n), BF16),
        scratch_shapes=[pltpu.VMEM((tm, k), BF16)],
        compiler_params=_params(2),
        name=name,
    )(x, g, w, col_scale)


def _matmul_residual_kernel(a_ref, w_ref, r_ref, o_ref):
    acc = jnp.dot(a_ref[...], w_ref[...], preferred_element_type=F32)
    o_ref[...] = r_ref[...] + acc


def _matmul_residual(a, w, res, *, tm, tn, name):
    m, k = a.shape
    n = w.shape[1]
    assert m % tm == 0 and n % tn == 0
    return pl.pallas_call(
        _matmul_residual_kernel,
        grid=(m // tm, n // tn),
        in_specs=[
            pl.BlockSpec((tm, k), lambda i, j: (i, 0)),
            pl.BlockSpec((k, tn), lambda i, j: (0, j)),
            pl.BlockSpec((tm, tn), lambda i, j: (i, j)),
        ],
        out_specs=pl.BlockSpec((tm, tn), lambda i, j: (i, j)),
        out_shape=jax.ShapeDtypeStruct((m, n), F32),
        compiler_params=_params(2),
        name=name,
    )(a, w, res)


def _diff_attn_kernel(q_ref, k_ref, v_ref, lq1_ref, lk1_ref, lq2_ref, lk2_ref,
                      sub_ref, o_ref, m_ref, l_ref, acc_ref, *, lam_init):
    d = DIFF_HEAD_DIM
    tq = q_ref.shape[0]
    ck = ATT_CK
    n_band = tq // ck
    qi = pl.program_id(2)

    m_ref[...] = jnp.full(m_ref.shape, NEG_INF, F32)
    l_ref[...] = jnp.zeros(l_ref.shape, F32)
    acc_ref[...] = jnp.zeros(acc_ref.shape, F32)

    def chunk(j, row0, masked):
        rows = tq - row0
        kstart = pl.multiple_of(j * ck, ck)
        v = v_ref[pl.ds(kstart, ck), :]
        if masked:
            r_id = lax.broadcasted_iota(jnp.int32, (rows, ck), 0)
            c_id = lax.broadcasted_iota(jnp.int32, (rows, ck), 1)
            visible = c_id <= r_id
        for half in range(2):
            q = q_ref[row0:, half * d:(half + 1) * d]
            k = k_ref[pl.ds(kstart, ck), half * d:(half + 1) * d]
            s = lax.dot_general(q, k, (((1,), (1,)), ((), ())),
                                preferred_element_type=F32)
            if masked:
                s = jnp.where(visible, s, NEG_INF)
            m_prev = m_ref[half, row0:, :]
            m_new = jnp.maximum(m_prev, jnp.max(s, axis=1, keepdims=True))
            alpha = jnp.exp2(m_prev - m_new)
            p = jnp.exp2(s - _lane_tile(m_new, ck // V7X_LANES))
            l_ref[half, row0:, :] = (alpha * l_ref[half, row0:, :]
                                     + jnp.sum(p, axis=1, keepdims=True))
            pv = jnp.dot(p.astype(BF16), v, preferred_element_type=F32)
            acc_ref[half, row0:, :] = (
                acc_ref[half, row0:, :] * _lane_tile(alpha, 2 * d // V7X_LANES) + pv)
            m_ref[half, row0:, :] = m_new

    def body(j, carry):
        chunk(j, 0, False)
        return carry

    lax.fori_loop(0, qi * n_band, body, 0)
    for c in range(n_band):
        chunk(qi * n_band + c, c * ck, True)

    lam = (jnp.exp(jnp.sum(lq1_ref[...] * lk1_ref[...], axis=1, keepdims=True))
           - jnp.exp(jnp.sum(lq2_ref[...] * lk2_ref[...], axis=1, keepdims=True))
           + lam_init)
    reps = 2 * d // V7X_LANES
    o1 = acc_ref[0] / _lane_tile(l_ref[0], reps)
    o2 = acc_ref[1] / _lane_tile(l_ref[1], reps)
    o = o1 - lam * o2
    o = _rmsnorm_f32(o, sub_ref[...]) * (1.0 - lam_init)
    o_ref[...] = o.astype(o_ref.dtype)


def _diff_attention(proj, lq1, lk1, lq2, lk2, subln, *, n_heads, lam_init):
    b, s, _ = proj.shape
    hd = 2 * DIFF_HEAD_DIM
    tq = ATT_TQ
    assert s % tq == 0 and tq % ATT_CK == 0
    vec = pl.BlockSpec((1, DIFF_HEAD_DIM), lambda bi, h, i: (0, 0))
    return pl.pallas_call(
        functools.partial(_diff_attn_kernel, lam_init=lam_init),
        grid=(b, n_heads, s // tq),
        in_specs=[
            pl.BlockSpec((None, tq, hd), lambda bi, h, i: (bi, i, h)),
            pl.BlockSpec((None, s, hd), lambda bi, h, i: (bi, 0, n_heads + h)),
            pl.BlockSpec((None, s, hd), lambda bi, h, i: (bi, 0, 2 * n_heads + h)),
            vec, vec, vec, vec,
            pl.BlockSpec((1, hd), lambda bi, h, i: (0, 0)),
        ],
        out_specs=pl.BlockSpec((None, tq, hd), lambda bi, h, i: (bi, i, h)),
        out_shape=jax.ShapeDtypeStruct((b, s, n_heads * hd), BF16),
        scratch_shapes=[
            pltpu.VMEM((2, tq, V7X_LANES), F32),
            pltpu.VMEM((2, tq, V7X_LANES), F32),
            pltpu.VMEM((2, tq, hd), F32),
        ],
        compiler_params=_params(3),
        name="diff_attention",
    )(proj, proj, proj, lq1, lk1, lq2, lk2, subln)


def _pool_out_kernel(u_ref, uh_ref, oa_ref, x_ref, pw_ref, ps_ref, wo_ref, o_ref,
                     e_ref, a1_ref, a2_ref, a3_ref):
    tm, c = u_ref.shape
    halo = POOL_HALO
    n = tm + halo
    gd = c // len(POOL_WINDOWS)
    i = pl.program_id(1)

    u = u_ref[...].astype(F32)
    hist = uh_ref[...].astype(F32)
    e_ref[:halo, :] = jnp.where(i == 0, 0.0, hist)
    e_ref[halo:, :] = u
    a1_ref[8:, :] = e_ref[8:, :] + e_ref[7:n - 1, :]
    a2_ref[16:, :] = a1_ref[16:, gd:] + a1_ref[14:n - 2, gd:]
    a3_ref[24:, :] = a2_ref[24:, gd:] + a2_ref[20:n - 4, gd:]
    a4 = a3_ref[halo:, gd:] + a3_ref[halo - 8:n - 8, gd:]
    wsum = (a1_ref[halo:, :gd], a2_ref[halo:, :gd], a3_ref[halo:, :gd], a4)

    pos1 = lax.broadcasted_iota(jnp.int32, (tm, gd), 0) + (i * tm + 1)
    mixed = []
    for g, wl in enumerate(POOL_WINDOWS):
        cnt = jnp.minimum(pos1, wl).astype(F32)
        pooled = wsum[g] / cnt - u[:, g * gd:(g + 1) * gd]
        mg = jnp.dot(pooled.astype(BF16), pw_ref[g], preferred_element_type=F32)
        mixed.append((mg * ps_ref[:, g * gd:(g + 1) * gd]).astype(BF16))
    mixed = jnp.concatenate(mixed, axis=1)

    da = oa_ref.shape[1]
    acc = jnp.dot(oa_ref[...], wo_ref[:da, :], preferred_element_type=F32)
    acc = acc + jnp.dot(mixed, wo_ref[da:, :], preferred_element_type=F32)
    o_ref[...] = x_ref[...] + acc


def _pool_out_proj(proj, o_attn, x, pool_w, pool_scale, w_o):
    b, s, d_model = x.shape
    d_attn = o_attn.shape[2]
    c = d_model - d_attn
    tm, halo = POOL_TM, POOL_HALO
    assert s % tm == 0 and tm % halo == 0 and proj.shape[2] == 3 * d_attn + c
    assert (3 * d_attn) % c == 0
    u_blk = 3 * d_attn // c
    hb = tm // halo
    gd = c // len(POOL_WINDOWS)
    n = tm + halo
    return pl.pallas_call(
        _pool_out_kernel,
        grid=(b, s // tm),
        in_specs=[
            pl.BlockSpec((None, tm, c), lambda bi, i: (bi, i, u_blk)),
            pl.BlockSpec((None, halo, c),
                         lambda bi, i: (bi, jnp.maximum(i * hb - 1, 0), u_blk)),
            pl.BlockSpec((None, tm, d_attn), lambda bi, i: (bi, i, 0)),
            pl.BlockSpec((None, tm, d_model), lambda bi, i: (bi, i, 0)),
            pl.BlockSpec(pool_w.shape, lambda bi, i: (0, 0, 0)),
            pl.BlockSpec((1, c), lambda bi, i: (0, 0)),
            pl.BlockSpec(w_o.shape, lambda bi, i: (0, 0)),
        ],
        out_specs=pl.BlockSpec((None, tm, d_model), lambda bi, i: (bi, i, 0)),
        out_shape=jax.ShapeDtypeStruct((b, s, d_model), F32),
        scratch_shapes=[
            pltpu.VMEM((n, c), F32),
            pltpu.VMEM((n, c), F32),
            pltpu.VMEM((n, c - gd), F32),
            pltpu.VMEM((n, c - 2 * gd), F32),
        ],
        compiler_params=_params(2),
        name="pool_out_proj",
    )(proj, proj, o_attn, x, pool_w, pool_scale, w_o)


def _cross_attn_kernel(x_ref, g_ref, wq_ref, kv_ref, o_ref):
    d_model = x_ref.shape[1]
    hd = d_model // N_XATTN_HEADS
    hx = _rmsnorm_f32(x_ref[...], g_ref[...]).astype(BF16)
    q = jnp.dot(hx, wq_ref[...], preferred_element_type=F32)
    q = (q * (hd ** -0.5 * LOG2E)).astype(BF16)
    for h in range(N_XATTN_HEADS):
        qh = q[:, h * hd:(h + 1) * hd]
        kh = kv_ref[:, h * hd:(h + 1) * hd]
        vh = kv_ref[:, d_model + h * hd:d_model + (h + 1) * hd]
        s = lax.dot_general(qh, kh, (((1,), (1,)), ((), ())),
                            preferred_element_type=F32)
        p = jnp.exp2(s - jnp.max(s, axis=1, keepdims=True))
        l = jnp.sum(p, axis=1, keepdims=True)
        oh = jnp.dot(p.astype(BF16), vh, preferred_element_type=F32) / l
        o_ref[:, h * hd:(h + 1) * hd] = oh.astype(o_ref.dtype)


def _cross_attention(x, g, wq, kv):
    b, s, d_model = x.shape
    tm = XATT_TM
    assert s % tm == 0
    return pl.pallas_call(
        _cross_attn_kernel,
        grid=(b, s // tm),
        in_specs=[
            pl.BlockSpec((None, tm, d_model), lambda bi, i: (bi, i, 0)),
            pl.BlockSpec((1, d_model), lambda bi, i: (0, 0)),
            pl.BlockSpec(wq.shape, lambda bi, i: (0, 0)),
            pl.BlockSpec((None,) + kv.shape[1:], lambda bi, i: (bi, 0, 0)),
        ],
        out_specs=pl.BlockSpec((None, tm, d_model), lambda bi, i: (bi, i, 0)),
        out_shape=jax.ShapeDtypeStruct((b, s, d_model), BF16),
        compiler_params=_params(2),
        name="cross_attention",
    )(x, g, wq, kv)


def _ffn_kernel(x_ref, g_ref, wg_ref, wu_ref, wd_ref, gf_ref, o_ref, h_ref, *,
                final_norm):
    j = pl.program_id(1)

    @pl.when(j == 0)
    def _():
        x = x_ref[...]
        h_ref[...] = _rmsnorm_f32(x, g_ref[...]).astype(h_ref.dtype)
        o_ref[...] = x

    h = h_ref[...]
    gate = jnp.dot(h, wg_ref[...], preferred_element_type=F32)
    up = jnp.dot(h, wu_ref[...], preferred_element_type=F32)
    act = (gate / (1.0 + jnp.exp(-gate)) * up).astype(BF16)
    o_ref[...] += jnp.dot(act, wd_ref[...], preferred_element_type=F32)

    if final_norm:
        @pl.when(j == pl.num_programs(1) - 1)
        def _():
            o_ref[...] = _rmsnorm_f32(o_ref[...], gf_ref[...])


def _ffn(x, g, w_gate, w_up, w_down, g_final, *, final_norm):
    m, d_model = x.shape
    f = w_gate.shape[1]
    tm, tf = FFN_TM, FFN_TF
    assert m % tm == 0 and f % tf == 0
    return pl.pallas_call(
        functools.partial(_ffn_kernel, final_norm=final_norm),
        grid=(m // tm, f // tf),
        in_specs=[
            pl.BlockSpec((tm, d_model), lambda i, j: (i, 0)),
            pl.BlockSpec((1, d_model), lambda i, j: (0, 0)),
            pl.BlockSpec((d_model, tf), lambda i, j: (0, j)),
            pl.BlockSpec((d_model, tf), lambda i, j: (0, j)),
            pl.BlockSpec((tf, d_model), lambda i, j: (j, 0)),
            pl.BlockSpec((1, d_model), lambda i, j: (0, 0)),
        ],
        out_specs=pl.BlockSpec((tm, d_model), lambda i, j: (i, 0)),
        out_shape=jax.ShapeDtypeStruct((m, d_model), F32),
        scratch_shapes=[pltpu.VMEM((tm, d_model), BF16)],
        compiler_params=_params(2),
        name="swiglu_final_norm",
    )(x, g, w_gate, w_up, w_down, g_final)


def kernel(x, mem, norm_mix, w_in, lambda_q1, lambda_k1, lambda_q2, lambda_k2, subln, pool_w, pool_scale, w_o, norm_xattn, norm_mem, wq_x, wkv_x, wo_x, norm_ffn, w_gate, w_up, w_down, norm_final):
    b, s, d_model = x.shape
    depth = norm_mix.shape[0]
    d_attn = d_model // 2
    n_diff_heads = d_attn // (2 * DIFF_HEAD_DIM)
    d_in = w_in.shape[2]
    t = b * s

    col_scale = jnp.concatenate([
        jnp.full((1, d_attn), DIFF_HEAD_DIM ** -0.5 * LOG2E, F32),
        jnp.ones((1, d_in - d_attn), F32)], axis=1)
    ones_kv = jnp.ones((1, 2 * d_model), F32)
    row = lambda v: v.reshape(1, -1).astype(F32)

    for l in range(depth):
        lam_init = _lambda_init(l)
        proj = _norm_matmul(x.reshape(t, d_model), row(norm_mix[l]),
                            w_in[l].astype(BF16), col_scale,
                            tm=MM_TM, tn=MM_TN, name="in_proj")
        proj = proj.reshape(b, s, d_in)
        o_attn = _diff_attention(proj, row(lambda_q1[l]), row(lambda_k1[l]),
                                 row(lambda_q2[l]), row(lambda_k2[l]), row(subln[l]),
                                 n_heads=n_diff_heads, lam_init=lam_init)
        x = _pool_out_proj(proj, o_attn, x, pool_w[l].astype(BF16),
                           row(pool_scale[l]), w_o[l].astype(BF16))
        kv = _norm_matmul(mem.reshape(b * MEM_LEN, d_model), row(norm_mem[l]),
                          wkv_x[l].astype(BF16), ones_kv,
                          tm=b * MEM_LEN, tn=MM_TN, name="mem_kv_proj")
        o_x = _cross_attention(x, row(norm_xattn[l]), wq_x[l].astype(BF16),
                               kv.reshape(b, MEM_LEN, 2 * d_model))
        x = _matmul_residual(o_x.reshape(t, d_model), wo_x[l].astype(BF16),
                             x.reshape(t, d_model), tm=MM_TM, tn=MM_TN,
                             name="xattn_out_proj").reshape(b, s, d_model)
        x = _ffn(x.reshape(t, d_model), row(norm_ffn[l]), w_gate[l].astype(BF16),
                 w_up[l].astype(BF16), w_down[l].astype(BF16), row(norm_final),
                 final_norm=(l == depth - 1)).reshape(b, s, d_model)
    return x
```

```python
import functools
import math

import jax
import jax.numpy as jnp
from jax import lax
from jax.experimental import pallas as pl
from jax.experimental.pallas import tpu as pltpu

MEM_LEN = 256
DIFF_HEAD_DIM = 128
POOL_WINDOWS = (2, 4, 8, 16)
N_XATTN_HEADS = 4
EPS = 1e-6
NEG_INF = -1e30
LOG2E = math.log2(math.e)

V7X_VMEM_BYTES = 64 * 1024 * 1024
V7X_LANES = 128
V7X_SUBLANES = 8
VMEM_LIMIT_BYTES = V7X_VMEM_BYTES - 8 * 1024 * 1024

MM_TM = 512
MM_TN = 1024
ATT_TQ = 1024
ATT_CK = 512
POOL_TM = 512
POOL_HALO = 32
XATT_TM = 512
FFN_TM = 1024
FFN_TF = 512

BF16 = jnp.bfloat16
F32 = jnp.float32


def _lambda_init(layer_idx):
    return 0.8 - 0.6 * math.exp(-0.3 * layer_idx)


def _rmsnorm_f32(x, g):
    y = x * lax.rsqrt(jnp.mean(x * x, axis=-1, keepdims=True) + EPS)
    return y * g


def _lane_tile(x, reps):
    return x if reps == 1 else jnp.concatenate([x] * reps, axis=1)


def _params(n_axes):
    return pltpu.CompilerParams(
        dimension_semantics=("arbitrary",) * n_axes,
        vmem_limit_bytes=VMEM_LIMIT_BYTES,
    )


def _resident(shape):
    nd = len(shape)
    return pl.BlockSpec(shape, lambda *_: (0,) * nd, pipeline_mode=pl.Buffered(1))


def _norm_matmul_kernel(x_ref, g_ref, w_ref, cs_ref, o_ref, *, tn):
    h = _rmsnorm_f32(x_ref[...], g_ref[...]).astype(BF16)
    for c0 in range(0, o_ref.shape[1], tn):
        acc = jnp.dot(h, w_ref[:, c0:c0 + tn], preferred_element_type=F32)
        o_ref[:, c0:c0 + tn] = (acc * cs_ref[:, c0:c0 + tn]).astype(o_ref.dtype)


def _norm_matmul(x, g, w, col_scale, *, tm, tn, name):
    m, k = x.shape
    n = w.shape[1]
    assert m % tm == 0 and n % tn == 0
    return pl.pallas_call(
        functools.partial(_norm_matmul_kernel, tn=tn),
        grid=(m // tm,),
        in_specs=[
            pl.BlockSpec((tm, k), lambda i: (i, 0)),
            _resident((1, k)),
            _resident((k, n)),
            _resident((1, n)),
        ],
        out_specs=pl.BlockSpec((tm, n), lambda i: (i, 0)),
        out_shape=jax.ShapeDtypeStruct((m, n), BF16),
        compiler_params=_params(1),
        name=name,
    )(x, g, w, col_scale)


def _diff_attn_kernel(q_ref, k_ref, v_ref, lq1_ref, lk1_ref, lq2_ref, lk2_ref,
                      sub_ref, o_ref, m_ref, l_ref, acc_ref, s_ref, p_ref, a_ref, *,
                      lam_init):
    d = DIFF_HEAD_DIM
    tq = q_ref.shape[0]
    ck = ATT_CK
    n_band = tq // ck
    qi = pl.program_id(2)
    n_full = qi * n_band

    m_ref[...] = jnp.full(m_ref.shape, NEG_INF, F32)
    l_ref[...] = jnp.zeros(l_ref.shape, F32)
    acc_ref[...] = jnp.zeros(acc_ref.shape, F32)

    def scores(j, slot, row0):
        kstart = pl.multiple_of(j * ck, ck)
        for half in range(2):
            q = q_ref[row0:, half * d:(half + 1) * d]
            k = k_ref[pl.ds(kstart, ck), half * d:(half + 1) * d]
            s_ref[slot, half, row0:, :] = lax.dot_general(
                q, k, (((1,), (1,)), ((), ())), preferred_element_type=F32)

    def softmax(slot, row0, masked):
        rows = tq - row0
        if masked:
            r_id = lax.broadcasted_iota(jnp.int32, (rows, ck), 0)
            c_id = lax.broadcasted_iota(jnp.int32, (rows, ck), 1)
            visible = c_id <= r_id
        for half in range(2):
            s = s_ref[slot, half, row0:, :]
            if masked:
                s = jnp.where(visible, s, NEG_INF)
            m_prev = m_ref[half, row0:, :]
            m_new = jnp.maximum(m_prev, jnp.max(s, axis=1, keepdims=True))
            alpha = jnp.exp2(m_prev - m_new)
            p = jnp.exp2(s - _lane_tile(m_new, ck // V7X_LANES))
            l_ref[half, row0:, :] = (alpha * l_ref[half, row0:, :]
                                     + jnp.sum(p, axis=1, keepdims=True))
            m_ref[half, row0:, :] = m_new
            a_ref[slot, half, row0:, :] = alpha
            p_ref[slot, half, row0:, :] = p.astype(BF16)

    def values(j, slot, row0):
        kstart = pl.multiple_of(j * ck, ck)
        v = v_ref[pl.ds(kstart, ck), :]
        for half in range(2):
            pv = jnp.dot(p_ref[slot, half, row0:, :], v, preferred_element_type=F32)
            alpha = _lane_tile(a_ref[slot, half, row0:, :], 2 * d // V7X_LANES)
            acc_ref[half, row0:, :] = acc_ref[half, row0:, :] * alpha + pv

    for i in range(n_band):
        band = n_band - 1 - i
        scores(n_full + band, i % 2, band * ck)
        if i > 0:
            values(n_full + band + 1, (i - 1) % 2, (band + 1) * ck)
        softmax(i % 2, band * ck, True)
    last_band_slot = (n_band - 1) % 2

    @pl.when(qi == 0)
    def _():
        values(n_full, last_band_slot, 0)

    @pl.when(qi > 0)
    def _():
        scores(0, 0, 0)
        values(n_full, last_band_slot, 0)
        softmax(0, 0, False)
        scores(1, 1, 0)

        def body(t, carry):
            j = 1 + 2 * t
            values(j - 1, 0, 0)
            softmax(1, 0, False)
            scores(j + 1, 0, 0)
            values(j, 1, 0)
            softmax(0, 0, False)
            scores(j + 2, 1, 0)
            return carry

        lax.fori_loop(0, qi * (n_band // 2) - 1, body, 0)
        values(n_full - 2, 0, 0)
        softmax(1, 0, False)
        values(n_full - 1, 1, 0)

    lam = (jnp.exp(jnp.sum(lq1_ref[...] * lk1_ref[...], axis=1, keepdims=True))
           - jnp.exp(jnp.sum(lq2_ref[...] * lk2_ref[...], axis=1, keepdims=True))
           + lam_init)
    reps = 2 * d // V7X_LANES
    o1 = acc_ref[0] / _lane_tile(l_ref[0], reps)
    o2 = acc_ref[1] / _lane_tile(l_ref[1], reps)
    o = o1 - lam * o2
    o = _rmsnorm_f32(o, sub_ref[...]) * (1.0 - lam_init)
    o_ref[...] = o.astype(o_ref.dtype)


def _diff_attention(proj, lq1, lk1, lq2, lk2, subln, *, n_heads, lam_init):
    b, s, _ = proj.shape
    hd = 2 * DIFF_HEAD_DIM
    tq = ATT_TQ
    assert s % tq == 0 and tq % (2 * ATT_CK) == 0
    vec = pl.BlockSpec((1, DIFF_HEAD_DIM), lambda bi, h, i: (0, 0))
    return pl.pallas_call(
        functools.partial(_diff_attn_kernel, lam_init=lam_init),
        grid=(b, n_heads, s // tq),
        in_specs=[
            pl.BlockSpec((None, tq, hd), lambda bi, h, i: (bi, i, h)),
            pl.BlockSpec((None, s, hd), lambda bi, h, i: (bi, 0, n_heads + h)),
            pl.BlockSpec((None, s, hd), lambda bi, h, i: (bi, 0, 2 * n_heads + h)),
            vec, vec, vec, vec,
            pl.BlockSpec((1, hd), lambda bi, h, i: (0, 0)),
        ],
        out_specs=pl.BlockSpec((None, tq, hd), lambda bi, h, i: (bi, i, h)),
        out_shape=jax.ShapeDtypeStruct((b, s, n_heads * hd), BF16),
        scratch_shapes=[
            pltpu.VMEM((2, tq, V7X_LANES), F32),
            pltpu.VMEM((2, tq, V7X_LANES), F32),
            pltpu.VMEM((2, tq, hd), F32),
            pltpu.VMEM((2, 2, tq, ATT_CK), F32),
            pltpu.VMEM((2, 2, tq, ATT_CK), BF16),
            pltpu.VMEM((2, 2, tq, V7X_LANES), F32),
        ],
        compiler_params=_params(3),
        name="diff_attention",
    )(proj, proj, proj, lq1, lk1, lq2, lk2, subln)


def _pool_out_kernel(u_ref, uh_ref, oa_ref, x_ref, pw_ref, ps_ref, wo_ref, o_ref,
                     e_ref, a1_ref, a2_ref, a3_ref):
    tm, c = u_ref.shape
    halo = POOL_HALO
    n = tm + halo
    gd = c // len(POOL_WINDOWS)
    i = pl.program_id(1)

    u = u_ref[...].astype(F32)
    hist = uh_ref[...].astype(F32)
    e_ref[:halo, :] = jnp.where(i == 0, 0.0, hist)
    e_ref[halo:, :] = u
    a1_ref[8:, :] = e_ref[8:, :] + e_ref[7:n - 1, :]
    a2_ref[16:, :] = a1_ref[16:, gd:] + a1_ref[14:n - 2, gd:]
    a3_ref[24:, :] = a2_ref[24:, gd:] + a2_ref[20:n - 4, gd:]
    a4 = a3_ref[halo:, gd:] + a3_ref[halo - 8:n - 8, gd:]
    wsum = (a1_ref[halo:, :gd], a2_ref[halo:, :gd], a3_ref[halo:, :gd], a4)

    pos1 = lax.broadcasted_iota(jnp.int32, (tm, gd), 0) + (i * tm + 1)
    mixed = []
    for g, wl in enumerate(POOL_WINDOWS):
        cnt = jnp.minimum(pos1, wl).astype(F32)
        pooled = wsum[g] / cnt - u[:, g * gd:(g + 1) * gd]
        mg = jnp.dot(pooled.astype(BF16), pw_ref[g], preferred_element_type=F32)
        mixed.append((mg * ps_ref[:, g * gd:(g + 1) * gd]).astype(BF16))
    mixed = jnp.concatenate(mixed, axis=1)

    da = oa_ref.shape[1]
    acc = jnp.dot(oa_ref[...], wo_ref[:da, :], preferred_element_type=F32)
    acc = acc + jnp.dot(mixed, wo_ref[da:, :], preferred_element_type=F32)
    o_ref[...] = x_ref[...] + acc


def _pool_out_proj(proj, o_attn, x, pool_w, pool_scale, w_o):
    b, s, d_model = x.shape
    d_attn = o_attn.shape[2]
    c = d_model - d_attn
    tm, halo = POOL_TM, POOL_HALO
    assert s % tm == 0 and tm % halo == 0 and proj.shape[2] == 3 * d_attn + c
    assert (3 * d_attn) % c == 0
    u_blk = 3 * d_attn // c
    hb = tm // halo
    gd = c // len(POOL_WINDOWS)
    n = tm + halo
    return pl.pallas_call(
        _pool_out_kernel,
        grid=(b, s // tm),
        in_specs=[
            pl.BlockSpec((None, tm, c), lambda bi, i: (bi, i, u_blk)),
            pl.BlockSpec((None, halo, c),
                         lambda bi, i: (bi, jnp.maximum(i * hb - 1, 0), u_blk)),
            pl.BlockSpec((None, tm, d_attn), lambda bi, i: (bi, i, 0)),
            pl.BlockSpec((None, tm, d_model), lambda bi, i: (bi, i, 0)),
            _resident(pool_w.shape),
            _resident((1, c)),
            _resident(w_o.shape),
        ],
        out_specs=pl.BlockSpec((None, tm, d_model), lambda bi, i: (bi, i, 0)),
        out_shape=jax.ShapeDtypeStruct((b, s, d_model), F32),
        scratch_shapes=[
            pltpu.VMEM((n, c), F32),
            pltpu.VMEM((n, c), F32),
            pltpu.VMEM((n, c - gd), F32),
            pltpu.VMEM((n, c - 2 * gd), F32),
        ],
        compiler_params=_params(2),
        name="pool_out_proj",
    )(proj, proj, o_attn, x, pool_w, pool_scale, w_o)


def _cross_attn_kernel(x_ref, g_ref, wq_ref, kv_ref, wo_ref, o_ref, oh_ref):
    d_model = x_ref.shape[1]
    hd = d_model // N_XATTN_HEADS
    x = x_ref[...]
    hx = _rmsnorm_f32(x, g_ref[...]).astype(BF16)
    for h in range(N_XATTN_HEADS):
        q = jnp.dot(hx, wq_ref[:, h * hd:(h + 1) * hd], preferred_element_type=F32)
        q = (q * (hd ** -0.5 * LOG2E)).astype(BF16)
        kh = kv_ref[:, h * hd:(h + 1) * hd]
        vh = kv_ref[:, d_model + h * hd:d_model + (h + 1) * hd]
        s = lax.dot_general(q, kh, (((1,), (1,)), ((), ())),
                            preferred_element_type=F32)
        p = jnp.exp2(s - jnp.max(s, axis=1, keepdims=True))
        l = jnp.sum(p, axis=1, keepdims=True)
        oh = jnp.dot(p.astype(BF16), vh, preferred_element_type=F32) / l
        oh_ref[:, h * hd:(h + 1) * hd] = oh.astype(oh_ref.dtype)
    o_ref[...] = x + jnp.dot(oh_ref[...], wo_ref[...], preferred_element_type=F32)


def _cross_attention(x, g, wq, kv, wo):
    b, s, d_model = x.shape
    tm = XATT_TM
    assert s % tm == 0
    return pl.pallas_call(
        _cross_attn_kernel,
        grid=(b, s // tm),
        in_specs=[
            pl.BlockSpec((None, tm, d_model), lambda bi, i: (bi, i, 0)),
            _resident((1, d_model)),
            _resident(wq.shape),
            pl.BlockSpec((None,) + kv.shape[1:], lambda bi, i: (bi, 0, 0)),
            _resident(wo.shape),
        ],
        out_specs=pl.BlockSpec((None, tm, d_model), lambda bi, i: (bi, i, 0)),
        out_shape=jax.ShapeDtypeStruct((b, s, d_model), F32),
        scratch_shapes=[pltpu.VMEM((tm, d_model), BF16)],
        compiler_params=_params(2),
        name="cross_attention",
    )(x, g, wq, kv, wo)


def _ffn_kernel(x_ref, g_ref, wg_ref, wu_ref, wd_ref, gf_ref, o_ref, h_ref, *,
                final_norm):
    j = pl.program_id(1)

    @pl.when(j == 0)
    def _():
        x = x_ref[...]
        h_ref[...] = _rmsnorm_f32(x, g_ref[...]).astype(h_ref.dtype)
        o_ref[...] = x

    h = h_ref[...]
    gate = jnp.dot(h, wg_ref[...], preferred_element_type=F32)
    up = jnp.dot(h, wu_ref[...], preferred_element_type=F32)
    act = (gate / (1.0 + jnp.exp(-gate)) * up).astype(BF16)
    o_ref[...] += jnp.dot(act, wd_ref[...], preferred_element_type=F32)

    if final_norm:
        @pl.when(j == pl.num_programs(1) - 1)
        def _():
            o_ref[...] = _rmsnorm_f32(o_ref[...], gf_ref[...])


def _ffn(x, g, w_gate, w_up, w_down, g_final, *, final_norm):
    m, d_model = x.shape
    f = w_gate.shape[1]
    tm, tf = FFN_TM, FFN_TF
    assert m % tm == 0 and f % tf == 0
    return pl.pallas_call(
        functools.partial(_ffn_kernel, final_norm=final_norm),
        grid=(m // tm, f // tf),
        in_specs=[
            pl.BlockSpec((tm, d_model), lambda i, j: (i, 0)),
            pl.BlockSpec((1, d_model), lambda i, j: (0, 0)),
            pl.BlockSpec((d_model, tf), lambda i, j: (0, j)),
            pl.BlockSpec((d_model, tf), lambda i, j: (0, j)),
            pl.BlockSpec((tf, d_model), lambda i, j: (j, 0)),
            pl.BlockSpec((1, d_model), lambda i, j: (0, 0)),
        ],
        out_specs=pl.BlockSpec((tm, d_model), lambda i, j: (i, 0)),
        out_shape=jax.ShapeDtypeStruct((m, d_model), F32),
        scratch_shapes=[pltpu.VMEM((tm, d_model), BF16)],
        compiler_params=_params(2),
        name="swiglu_final_norm",
    )(x, g, w_gate, w_up, w_down, g_final)


def kernel(x, mem, norm_mix, w_in, lambda_q1, lambda_k1, lambda_q2, lambda_k2, subln, pool_w, pool_scale, w_o, norm_xattn, norm_mem, wq_x, wkv_x, wo_x, norm_ffn, w_gate, w_up, w_down, norm_final):
    b, s, d_model = x.shape
    depth = norm_mix.shape[0]
    d_attn = d_model // 2
    n_diff_heads = d_attn // (2 * DIFF_HEAD_DIM)
    d_in = w_in.shape[2]
    t = b * s

    col_scale = jnp.concatenate([
        jnp.full((1, d_attn), DIFF_HEAD_DIM ** -0.5 * LOG2E, F32),
        jnp.ones((1, d_in - d_attn), F32)], axis=1)
    ones_kv = jnp.ones((1, 2 * d_model), F32)
    row = lambda v: v.reshape(1, -1).astype(F32)

    for l in range(depth):
        lam_init = _lambda_init(l)
        proj = _norm_matmul(x.reshape(t, d_model), row(norm_mix[l]),
                            w_in[l].astype(BF16), col_scale,
                            tm=MM_TM, tn=MM_TN, name="in_proj")
        proj = proj.reshape(b, s, d_in)
        o_attn = _diff_attention(proj, row(lambda_q1[l]), row(lambda_k1[l]),
                                 row(lambda_q2[l]), row(lambda_k2[l]), row(subln[l]),
                                 n_heads=n_diff_heads, lam_init=lam_init)
        x = _pool_out_proj(proj, o_attn, x, pool_w[l].astype(BF16),
                           row(pool_scale[l]), w_o[l].astype(BF16))
        kv = _norm_matmul(mem.reshape(b * MEM_LEN, d_model), row(norm_mem[l]),
                          wkv_x[l].astype(BF16), ones_kv,
                          tm=b * MEM_LEN, tn=MM_TN, name="mem_kv_proj")
        x = _cross_attention(x, row(norm_xattn[l]), wq_x[l].astype(BF16),
                             kv.reshape(b, MEM_LEN, 2 * d_model), wo_x[l].astype(BF16))
        x = _ffn(x.reshape(t, d_model), row(norm_ffn[l]), w_gate[l].astype(BF16),
                 w_up[l].astype(BF16), w_down[l].astype(BF16), row(norm_final),
                 final_norm=(l == depth - 1)).reshape(b, s, d_model)
    return x
```

```python
import functools
import math

import jax
import jax.numpy as jnp
from jax import lax
from jax.experimental import pallas as pl
from jax.experimental.pallas import tpu as pltpu

MEM_LEN = 256
DIFF_HEAD_DIM = 128
POOL_WINDOWS = (2, 4, 8, 16)
N_XATTN_HEADS = 4
EPS = 1e-6
NEG_INF = -1e30
LOG2E = math.log2(math.e)

V7X_VMEM_BYTES = 64 * 1024 * 1024
V7X_LANES = 128
V7X_SUBLANES = 8
VMEM_LIMIT_BYTES = V7X_VMEM_BYTES - 8 * 1024 * 1024

MM_TM = 512
MM_TN = 1024
ATT_TQ = 1024
ATT_CK = 512
POOL_TM = 512
POOL_HALO = 32
XATT_TM = 512
FFN_TM = 1024
FFN_TF = 512

BF16 = jnp.bfloat16
F32 = jnp.float32


def _lambda_init(layer_idx):
    return 0.8 - 0.6 * math.exp(-0.3 * layer_idx)


def _rmsnorm_f32(x, g):
    y = x * lax.rsqrt(jnp.mean(x * x, axis=-1, keepdims=True) + EPS)
    return y * g


def _lane_tile(x, reps):
    return x if reps == 1 else jnp.concatenate([x] * reps, axis=1)


def _params(n_axes):
    return pltpu.CompilerParams(
        dimension_semantics=("arbitrary",) * n_axes,
        vmem_limit_bytes=VMEM_LIMIT_BYTES,
    )


def _resident(shape):
    nd = len(shape)
    return pl.BlockSpec(shape, lambda *_: (0,) * nd, pipeline_mode=pl.Buffered(1))


def _norm_matmul_kernel(x_ref, g_ref, w_ref, cs_ref, o_ref, *, tn):
    h = _rmsnorm_f32(x_ref[...], g_ref[...]).astype(BF16)
    for c0 in range(0, o_ref.shape[1], tn):
        acc = jnp.dot(h, w_ref[:, c0:c0 + tn], preferred_element_type=F32)
        o_ref[:, c0:c0 + tn] = (acc * cs_ref[:, c0:c0 + tn]).astype(o_ref.dtype)


def _norm_matmul(x, g, w, col_scale, *, tm, tn, name):
    m, k = x.shape
    n = w.shape[1]
    assert m % tm == 0 and n % tn == 0
    return pl.pallas_call(
        functools.partial(_norm_matmul_kernel, tn=tn),
        grid=(m // tm,),
        in_specs=[
            pl.BlockSpec((tm, k), lambda i: (i, 0)),
            _resident((1, k)),
            _resident((k, n)),
            _resident((1, n)),
        ],
        out_specs=pl.BlockSpec((tm, n), lambda i: (i, 0)),
        out_shape=jax.ShapeDtypeStruct((m, n), BF16),
        compiler_params=_params(1),
        name=name,
    )(x, g, w, col_scale)


def _diff_attn_kernel(q_ref, k_ref, v_ref, lq1_ref, lk1_ref, lq2_ref, lk2_ref,
                      sub_ref, o_ref, m_ref, l_ref, acc_ref, p_ref, a_ref, *, lam_init):
    d = DIFF_HEAD_DIM
    tq = q_ref.shape[0]
    ck = ATT_CK
    n_band = tq // ck
    qi = pl.program_id(2)
    n_full = qi * n_band

    m_ref[...] = jnp.full(m_ref.shape, NEG_INF, F32)
    l_ref[...] = jnp.zeros(l_ref.shape, F32)
    acc_ref[...] = jnp.zeros(acc_ref.shape, F32)

    def probs(j, slot, row0, masked):
        rows = tq - row0
        kstart = pl.multiple_of(j * ck, ck)
        if masked:
            r_id = lax.broadcasted_iota(jnp.int32, (rows, ck), 0)
            c_id = lax.broadcasted_iota(jnp.int32, (rows, ck), 1)
            visible = c_id <= r_id
        for half in range(2):
            q = q_ref[row0:, half * d:(half + 1) * d]
            k = k_ref[pl.ds(kstart, ck), half * d:(half + 1) * d]
            s = lax.dot_general(q, k, (((1,), (1,)), ((), ())),
                                preferred_element_type=F32)
            if masked:
                s = jnp.where(visible, s, NEG_INF)
            m_prev = m_ref[half, row0:, :]
            m_new = jnp.maximum(m_prev, jnp.max(s, axis=1, keepdims=True))
            alpha = jnp.exp2(m_prev - m_new)
            p = jnp.exp2(s - _lane_tile(m_new, ck // V7X_LANES))
            l_ref[half, row0:, :] = (alpha * l_ref[half, row0:, :]
                                     + jnp.sum(p, axis=1, keepdims=True))
            m_ref[half, row0:, :] = m_new
            a_ref[slot, half, row0:, :] = alpha
            p_ref[slot, half, row0:, :] = p.astype(BF16)

    def values(j, slot, row0):
        kstart = pl.multiple_of(j * ck, ck)
        v = v_ref[pl.ds(kstart, ck), :]
        for half in range(2):
            pv = jnp.dot(p_ref[slot, half, row0:, :], v, preferred_element_type=F32)
            alpha = _lane_tile(a_ref[slot, half, row0:, :], 2 * d // V7X_LANES)
            acc_ref[half, row0:, :] = acc_ref[half, row0:, :] * alpha + pv

    for i in range(n_band):
        band = n_band - 1 - i
        probs(n_full + band, i % 2, band * ck, True)
        if i > 0:
            values(n_full + band + 1, (i - 1) % 2, (band + 1) * ck)

    def body(t, carry):
        j = 2 * t
        probs(j, 0, 0, False)
        values(jnp.where(t == 0, n_full, j - 1), 1, 0)
        probs(j + 1, 1, 0, False)
        values(j, 0, 0)
        return carry

    lax.fori_loop(0, qi * (n_band // 2), body, 0)
    values(jnp.where(n_full == 0, n_full, n_full - 1), 1, 0)

    lam = (jnp.exp(jnp.sum(lq1_ref[...] * lk1_ref[...], axis=1, keepdims=True))
           - jnp.exp(jnp.sum(lq2_ref[...] * lk2_ref[...], axis=1, keepdims=True))
           + lam_init)
    reps = 2 * d // V7X_LANES
    o1 = acc_ref[0] / _lane_tile(l_ref[0], reps)
    o2 = acc_ref[1] / _lane_tile(l_ref[1], reps)
    o = o1 - lam * o2
    o = _rmsnorm_f32(o, sub_ref[...]) * (1.0 - lam_init)
    o_ref[...] = o.astype(o_ref.dtype)


def _diff_attention(proj, lq1, lk1, lq2, lk2, subln, *, n_heads, lam_init):
    b, s, _ = proj.shape
    hd = 2 * DIFF_HEAD_DIM
    tq = ATT_TQ
    assert s % tq == 0 and tq % (2 * ATT_CK) == 0
    vec = pl.BlockSpec((1, DIFF_HEAD_DIM), lambda bi, h, i: (0, 0))
    return pl.pallas_call(
        functools.partial(_diff_attn_kernel, lam_init=lam_init),
        grid=(b, n_heads, s // tq),
        in_specs=[
            pl.BlockSpec((None, tq, hd), lambda bi, h, i: (bi, i, h)),
            pl.BlockSpec((None, s, hd), lambda bi, h, i: (bi, 0, n_heads + h)),
            pl.BlockSpec((None, s, hd), lambda bi, h, i: (bi, 0, 2 * n_heads + h)),
            vec, vec, vec, vec,
            pl.BlockSpec((1, hd), lambda bi, h, i: (0, 0)),
        ],
        out_specs=pl.BlockSpec((None, tq, hd), lambda bi, h, i: (bi, i, h)),
        out_shape=jax.ShapeDtypeStruct((b, s, n_heads * hd), BF16),
        scratch_shapes=[
            pltpu.VMEM((2, tq, V7X_LANES), F32),
            pltpu.VMEM((2, tq, V7X_LANES), F32),
            pltpu.VMEM((2, tq, hd), F32),
            pltpu.VMEM((2, 2, tq, ATT_CK), BF16),
            pltpu.VMEM((2, 2, tq, V7X_LANES), F32),
        ],
        compiler_params=_params(3),
        name="diff_attention",
    )(proj, proj, proj, lq1, lk1, lq2, lk2, subln)


def _pool_out_kernel(u_ref, uh_ref, oa_ref, x_ref, pw_ref, ps_ref, wo_ref, o_ref,
                     e_ref, a1_ref, a2_ref, a3_ref):
    tm, c = u_ref.shape
    halo = POOL_HALO
    n = tm + halo
    gd = c // len(POOL_WINDOWS)
    i = pl.program_id(1)

    u = u_ref[...].astype(F32)
    hist = uh_ref[...].astype(F32)
    e_ref[:halo, :] = jnp.where(i == 0, 0.0, hist)
    e_ref[halo:, :] = u
    a1_ref[8:, :] = e_ref[8:, :] + e_ref[7:n - 1, :]
    a2_ref[16:, :] = a1_ref[16:, gd:] + a1_ref[14:n - 2, gd:]
    a3_ref[24:, :] = a2_ref[24:, gd:] + a2_ref[20:n - 4, gd:]
    a4 = a3_ref[halo:, gd:] + a3_ref[halo - 8:n - 8, gd:]
    wsum = (a1_ref[halo:, :gd], a2_ref[halo:, :gd], a3_ref[halo:, :gd], a4)

    pos1 = lax.broadcasted_iota(jnp.int32, (tm, gd), 0) + (i * tm + 1)
    mixed = []
    for g, wl in enumerate(POOL_WINDOWS):
        cnt = jnp.minimum(pos1, wl).astype(F32)
        pooled = wsum[g] / cnt - u[:, g * gd:(g + 1) * gd]
        mg = jnp.dot(pooled.astype(BF16), pw_ref[g], preferred_element_type=F32)
        mixed.append((mg * ps_ref[:, g * gd:(g + 1) * gd]).astype(BF16))
    mixed = jnp.concatenate(mixed, axis=1)

    da = oa_ref.shape[1]
    acc = jnp.dot(oa_ref[...], wo_ref[:da, :], preferred_element_type=F32)
    acc = acc + jnp.dot(mixed, wo_ref[da:, :], preferred_element_type=F32)
    o_ref[...] = x_ref[...] + acc


def _pool_out_proj(proj, o_attn, x, pool_w, pool_scale, w_o):
    b, s, d_model = x.shape
    d_attn = o_attn.shape[2]
    c = d_model - d_attn
    tm, halo = POOL_TM, POOL_HALO
    assert s % tm == 0 and tm % halo == 0 and proj.shape[2] == 3 * d_attn + c
    assert (3 * d_attn) % c == 0
    u_blk = 3 * d_attn // c
    hb = tm // halo
    gd = c // len(POOL_WINDOWS)
    n = tm + halo
    return pl.pallas_call(
        _pool_out_kernel,
        grid=(b, s // tm),
        in_specs=[
            pl.BlockSpec((None, tm, c), lambda bi, i: (bi, i, u_blk)),
            pl.BlockSpec((None, halo, c),
                         lambda bi, i: (bi, jnp.maximum(i * hb - 1, 0), u_blk)),
            pl.BlockSpec((None, tm, d_attn), lambda bi, i: (bi, i, 0)),
            pl.BlockSpec((None, tm, d_model), lambda bi, i: (bi, i, 0)),
            _resident(pool_w.shape),
            _resident((1, c)),
            _resident(w_o.shape),
        ],
        out_specs=pl.BlockSpec((None, tm, d_model), lambda bi, i: (bi, i, 0)),
        out_shape=jax.ShapeDtypeStruct((b, s, d_model), F32),
        scratch_shapes=[
            pltpu.VMEM((n, c), F32),
            pltpu.VMEM((n, c), F32),
            pltpu.VMEM((n, c - gd), F32),
            pltpu.VMEM((n, c - 2 * gd), F32),
        ],
        compiler_params=_params(2),
        name="pool_out_proj",
    )(proj, proj, o_attn, x, pool_w, pool_scale, w_o)


def _cross_attn_kernel(x_ref, g_ref, wq_ref, kv_ref, wo_ref, o_ref, oh_ref):
    d_model = x_ref.shape[1]
    hd = d_model // N_XATTN_HEADS
    x = x_ref[...]
    hx = _rmsnorm_f32(x, g_ref[...]).astype(BF16)
    for h in range(N_XATTN_HEADS):
        q = jnp.dot(hx, wq_ref[:, h * hd:(h + 1) * hd], preferred_element_type=F32)
        q = (q * (hd ** -0.5 * LOG2E)).astype(BF16)
        kh = kv_ref[:, h * hd:(h + 1) * hd]
        vh = kv_ref[:, d_model + h * hd:d_model + (h + 1) * hd]
        s = lax.dot_general(q, kh, (((1,), (1,)), ((), ())),
                            preferred_element_type=F32)
        p = jnp.exp2(s - jnp.max(s, axis=1, keepdims=True))
        l = jnp.sum(p, axis=1, keepdims=True)
        oh = jnp.dot(p.astype(BF16), vh, preferred_element_type=F32) / l
        oh_ref[:, h * hd:(h + 1) * hd] = oh.astype(oh_ref.dtype)
    o_ref[...] = x + jnp.dot(oh_ref[...], wo_ref[...], preferred_element_type=F32)


def _cross_attention(x, g, wq, kv, wo):
    b, s, d_model = x.shape
    tm = XATT_TM
    assert s % tm == 0
    return pl.pallas_call(
        _cross_attn_kernel,
        grid=(b, s // tm),
        in_specs=[
            pl.BlockSpec((None, tm, d_model), lambda bi, i: (bi, i, 0)),
            _resident((1, d_model)),
            _resident(wq.shape),
            pl.BlockSpec((None,) + kv.shape[1:], lambda bi, i: (bi, 0, 0)),
            _resident(wo.shape),
        ],
        out_specs=pl.BlockSpec((None, tm, d_model), lambda bi, i: (bi, i, 0)),
        out_shape=jax.ShapeDtypeStruct((b, s, d_model), F32),
        scratch_shapes=[pltpu.VMEM((tm, d_model), BF16)],
        compiler_params=_params(2),
        name="cross_attention",
    )(x, g, wq, kv, wo)


def _ffn_kernel(x_ref, g_ref, wg_ref, wu_ref, wd_ref, gf_ref, o_ref, h_ref, *,
                final_norm):
    j = pl.program_id(1)

    @pl.when(j == 0)
    def _():
        x = x_ref[...]
        h_ref[...] = _rmsnorm_f32(x, g_ref[...]).astype(h_ref.dtype)
        o_ref[...] = x

    h = h_ref[...]
    gate = jnp.dot(h, wg_ref[...], preferred_element_type=F32)
    up = jnp.dot(h, wu_ref[...], preferred_element_type=F32)
    act = (gate / (1.0 + jnp.exp(-gate)) * up).astype(BF16)
    o_ref[...] += jnp.dot(act, wd_ref[...], preferred_element_type=F32)

    if final_norm:
        @pl.when(j == pl.num_programs(1) - 1)
        def _():
            o_ref[...] = _rmsnorm_f32(o_ref[...], gf_ref[...])


def _ffn(x, g, w_gate, w_up, w_down, g_final, *, final_norm):
    m, d_model = x.shape
    f = w_gate.shape[1]
    tm, tf = FFN_TM, FFN_TF
    assert m % tm == 0 and f % tf == 0
    return pl.pallas_call(
        functools.partial(_ffn_kernel, final_norm=final_norm),
        grid=(m // tm, f // tf),
        in_specs=[
            pl.BlockSpec((tm, d_model), lambda i, j: (i, 0)),
            pl.BlockSpec((1, d_model), lambda i, j: (0, 0)),
            pl.BlockSpec((d_model, tf), lambda i, j: (0, j)),
            pl.BlockSpec((d_model, tf), lambda i, j: (0, j)),
            pl.BlockSpec((tf, d_model), lambda i, j: (j, 0)),
            pl.BlockSpec((1, d_model), lambda i, j: (0, 0)),
        ],
        out_specs=pl.BlockSpec((tm, d_model), lambda i, j: (i, 0)),
        out_shape=jax.ShapeDtypeStruct((m, d_model), F32),
        scratch_shapes=[pltpu.VMEM((tm, d_model), BF16)],
        compiler_params=_params(2),
        name="swiglu_final_norm",
    )(x, g, w_gate, w_up, w_down, g_final)


def kernel(x, mem, norm_mix, w_in, lambda_q1, lambda_k1, lambda_q2, lambda_k2, subln, pool_w, pool_scale, w_o, norm_xattn, norm_mem, wq_x, wkv_x, wo_x, norm_ffn, w_gate, w_up, w_down, norm_final):
    b, s, d_model = x.shape
    depth = norm_mix.shape[0]
    d_attn = d_model // 2
    n_diff_heads = d_attn // (2 * DIFF_HEAD_DIM)
    d_in = w_in.shape[2]
    t = b * s

    col_scale = jnp.concatenate([
        jnp.full((1, d_attn), DIFF_HEAD_DIM ** -0.5 * LOG2E, F32),
        jnp.ones((1, d_in - d_attn), F32)], axis=1)
    ones_kv = jnp.ones((1, 2 * d_model), F32)
    row = lambda v: v.reshape(1, -1).astype(F32)

    for l in range(depth):
        lam_init = _lambda_init(l)
        proj = _norm_matmul(x.reshape(t, d_model), row(norm_mix[l]),
                            w_in[l].astype(BF16), col_scale,
                            tm=MM_TM, tn=MM_TN, name="in_proj")
        proj = proj.reshape(b, s, d_in)
        o_attn = _diff_attention(proj, row(lambda_q1[l]), row(lambda_k1[l]),
                                 row(lambda_q2[l]), row(lambda_k2[l]), row(subln[l]),
                                 n_heads=n_diff_heads, lam_init=lam_init)
        x = _pool_out_proj(proj, o_attn, x, pool_w[l].astype(BF16),
                           row(pool_scale[l]), w_o[l].astype(BF16))
        kv = _norm_matmul(mem.reshape(b * MEM_LEN, d_model), row(norm_mem[l]),
                          wkv_x[l].astype(BF16), ones_kv,
                          tm=b * MEM_LEN, tn=MM_TN, name="mem_kv_proj")
        x = _cross_attention(x, row(norm_xattn[l]), wq_x[l].astype(BF16),
                             kv.reshape(b, MEM_LEN, 2 * d_model), wo_x[l].astype(BF16))
        x = _ffn(x.reshape(t, d_model), row(norm_ffn[l]), w_gate[l].astype(BF16),
                 w_up[l].astype(BF16), w_down[l].astype(BF16), row(norm_final),
                 final_norm=(l == depth - 1)).reshape(b, s, d_model)
    return x
```

```python
import functools
import math

import jax
import jax.numpy as jnp
from jax import lax
from jax.experimental import pallas as pl
from jax.experimental.pallas import tpu as pltpu

MEM_LEN = 256
DIFF_HEAD_DIM = 128
POOL_WINDOWS = (2, 4, 8, 16)
N_XATTN_HEADS = 4
EPS = 1e-6
NEG_INF = -1e30
LOG2E = math.log2(math.e)

V7X_VMEM_BYTES = 64 * 1024 * 1024
V7X_LANES = 128
V7X_SUBLANES = 8
VMEM_LIMIT_BYTES = V7X_VMEM_BYTES - 8 * 1024 * 1024

MM_TM = 512
MM_TN = 1024
ATT_TQ = 1024
ATT_CK = 512
POOL_TM = 512
POOL_HALO = 32
XATT_TM = 512
FFN_TM = 1024
FFN_TF = 512

BF16 = jnp.bfloat16
F32 = jnp.float32


def _lambda_init(layer_idx):
    return 0.8 - 0.6 * math.exp(-0.3 * layer_idx)


def _rmsnorm_f32(x, g):
    y = x * lax.rsqrt(jnp.mean(x * x, axis=-1, keepdims=True) + EPS)
    return y * g


def _lane_tile(x, reps):
    return x if reps == 1 else jnp.concatenate([x] * reps, axis=1)


def _params(n_axes):
    return pltpu.CompilerParams(
        dimension_semantics=("arbitrary",) * n_axes,
        vmem_limit_bytes=VMEM_LIMIT_BYTES,
    )


def _resident(shape):
    nd = len(shape)
    return pl.BlockSpec(shape, lambda *_: (0,) * nd, pipeline_mode=pl.Buffered(1))


def _norm_matmul_kernel(x_ref, g_ref, w_ref, cs_ref, o_ref, *, tn):
    h = _rmsnorm_f32(x_ref[...], g_ref[...]).astype(BF16)
    for c0 in range(0, o_ref.shape[1], tn):
        acc = jnp.dot(h, w_ref[:, c0:c0 + tn], preferred_element_type=F32)
        o_ref[:, c0:c0 + tn] = (acc * cs_ref[:, c0:c0 + tn]).astype(o_ref.dtype)


def _norm_matmul(x, g, w, col_scale, *, tm, tn, name):
    m, k = x.shape
    n = w.shape[1]
    assert m % tm == 0 and n % tn == 0
    return pl.pallas_call(
        functools.partial(_norm_matmul_kernel, tn=tn),
        grid=(m // tm,),
        in_specs=[
            pl.BlockSpec((tm, k), lambda i: (i, 0)),
            _resident((1, k)),
            _resident((k, n)),
            _resident((1, n)),
        ],
        out_specs=pl.BlockSpec((tm, n), lambda i: (i, 0)),
        out_shape=jax.ShapeDtypeStruct((m, n), BF16),
        compiler_params=_params(1),
        name=name,
    )(x, g, w, col_scale)


def _diff_attn_kernel(q_ref, k_ref, v_ref, lq1_ref, lk1_ref, lq2_ref, lk2_ref,
                      sub_ref, o_ref, m_ref, l_ref, acc_ref, p_ref, a_ref, *, lam_init):
    d = DIFF_HEAD_DIM
    tq = q_ref.shape[0]
    ck = ATT_CK
    n_band = tq // ck
    qi = pl.program_id(2)
    n_full = qi * n_band

    m_ref[...] = jnp.full(m_ref.shape, NEG_INF, F32)
    l_ref[...] = jnp.zeros(l_ref.shape, F32)
    acc_ref[...] = jnp.zeros(acc_ref.shape, F32)

    def probs(j, slot, row0, masked):
        rows = tq - row0
        kstart = pl.multiple_of(j * ck, ck)
        if masked:
            r_id = lax.broadcasted_iota(jnp.int32, (rows, ck), 0)
            c_id = lax.broadcasted_iota(jnp.int32, (rows, ck), 1)
            visible = c_id <= r_id
        for half in range(2):
            q = q_ref[row0:, half * d:(half + 1) * d]
            k = k_ref[pl.ds(kstart, ck), half * d:(half + 1) * d]
            s = lax.dot_general(q, k, (((1,), (1,)), ((), ())),
                                preferred_element_type=F32)
            if masked:
                s = jnp.where(visible, s, NEG_INF)
            m_prev = m_ref[half, row0:, :]
            m_new = jnp.maximum(m_prev, jnp.max(s, axis=1, keepdims=True))
            alpha = jnp.exp2(m_prev - m_new)
            p = jnp.exp2(s - _lane_tile(m_new, ck // V7X_LANES))
            l_ref[half, row0:, :] = (alpha * l_ref[half, row0:, :]
                                     + jnp.sum(p, axis=1, keepdims=True))
            m_ref[half, row0:, :] = m_new
            a_ref[slot, half, row0:, :] = alpha
            p_ref[slot, half, row0:, :] = p.astype(BF16)

    def values(j, slot, row0):
        kstart = pl.multiple_of(j * ck, ck)
        v = v_ref[pl.ds(kstart, ck), :]
        for half in range(2):
            pv = jnp.dot(p_ref[slot, half, row0:, :], v, preferred_element_type=F32)
            alpha = _lane_tile(a_ref[slot, half, row0:, :], 2 * d // V7X_LANES)
            acc_ref[half, row0:, :] = acc_ref[half, row0:, :] * alpha + pv

    for i in range(n_band):
        band = n_band - 1 - i
        probs(n_full + band, i % 2, band * ck, True)
        if i > 0:
            values(n_full + band + 1, (i - 1) % 2, (band + 1) * ck)

    def body(t, carry):
        j = 2 * t
        probs(j, 0, 0, False)
        values(jnp.where(t == 0, n_full, j - 1), 1, 0)
        probs(j + 1, 1, 0, False)
        values(j, 0, 0)
        return carry

    lax.fori_loop(0, qi * (n_band // 2), body, 0)
    values(jnp.where(n_full == 0, n_full, n_full - 1), 1, 0)

    lam = (jnp.exp(jnp.sum(lq1_ref[...] * lk1_ref[...], axis=1, keepdims=True))
           - jnp.exp(jnp.sum(lq2_ref[...] * lk2_ref[...], axis=1, keepdims=True))
           + lam_init)
    reps = 2 * d // V7X_LANES
    o1 = acc_ref[0] / _lane_tile(l_ref[0], reps)
    o2 = acc_ref[1] / _lane_tile(l_ref[1], reps)
    o = o1 - lam * o2
    o = _rmsnorm_f32(o, sub_ref[...]) * (1.0 - lam_init)
    o_ref[...] = o.astype(o_ref.dtype)


def _diff_attention(proj, lq1, lk1, lq2, lk2, subln, *, n_heads, lam_init):
    b, s, _ = proj.shape
    hd = 2 * DIFF_HEAD_DIM
    tq = ATT_TQ
    assert s % tq == 0 and tq % (2 * ATT_CK) == 0
    vec = pl.BlockSpec((1, DIFF_HEAD_DIM), lambda bi, h, i: (0, 0))
    return pl.pallas_call(
        functools.partial(_diff_attn_kernel, lam_init=lam_init),
        grid=(b, n_heads, s // tq),
        in_specs=[
            pl.BlockSpec((None, tq, hd), lambda bi, h, i: (bi, i, h)),
            pl.BlockSpec((None, s, hd), lambda bi, h, i: (bi, 0, n_heads + h)),
            pl.BlockSpec((None, s, hd), lambda bi, h, i: (bi, 0, 2 * n_heads + h)),
            vec, vec, vec, vec,
            pl.BlockSpec((1, hd), lambda bi, h, i: (0, 0)),
        ],
        out_specs=pl.BlockSpec((None, tq, hd), lambda bi, h, i: (bi, i, h)),
        out_shape=jax.ShapeDtypeStruct((b, s, n_heads * hd), BF16),
        scratch_shapes=[
            pltpu.VMEM((2, tq, V7X_LANES), F32),
            pltpu.VMEM((2, tq, V7X_LANES), F32),
            pltpu.VMEM((2, tq, hd), F32),
            pltpu.VMEM((2, 2, tq, ATT_CK), BF16),
            pltpu.VMEM((2, 2, tq, V7X_LANES), F32),
        ],
        compiler_params=_params(3),
        name="diff_attention",
    )(proj, proj, proj, lq1, lk1, lq2, lk2, subln)


def _pool_out_kernel(u_ref, uh_ref, oa_ref, x_ref, pw_ref, ps_ref, wo_ref, o_ref,
                     e_ref, a1_ref, a2_ref, a3_ref):
    tm, c = u_ref.shape
    halo = POOL_HALO
    n = tm + halo
    gd = c // len(POOL_WINDOWS)
    i = pl.program_id(1)

    u = u_ref[...].astype(F32)
    hist = uh_ref[...].astype(F32)
    e_ref[:halo, :] = jnp.where(i == 0, 0.0, hist)
    e_ref[halo:, :] = u
    a1_ref[8:, :] = e_ref[8:, :] + e_ref[7:n - 1, :]
    a2_ref[16:, :] = a1_ref[16:, gd:] + a1_ref[14:n - 2, gd:]
    a3_ref[24:, :] = a2_ref[24:, gd:] + a2_ref[20:n - 4, gd:]
    a4 = a3_ref[halo:, gd:] + a3_ref[halo - 8:n - 8, gd:]
    wsum = (a1_ref[halo:, :gd], a2_ref[halo:, :gd], a3_ref[halo:, :gd], a4)

    head = max(POOL_WINDOWS)
    pos1 = lax.broadcasted_iota(jnp.int32, (head, gd), 0) + (i * tm + 1)
    mixed = []
    for g, wl in enumerate(POOL_WINDOWS):
        ug = u[:, g * gd:(g + 1) * gd]
        inv_head = 1.0 / jnp.minimum(pos1, wl).astype(F32)
        pooled = jnp.concatenate([wsum[g][:head] * inv_head - ug[:head],
                                  wsum[g][head:] * (1.0 / wl) - ug[head:]], axis=0)
        mg = jnp.dot(pooled.astype(BF16), pw_ref[g], preferred_element_type=F32)
        mixed.append((mg * ps_ref[:, g * gd:(g + 1) * gd]).astype(BF16))
    mixed = jnp.concatenate(mixed, axis=1)

    da = oa_ref.shape[1]
    acc = jnp.dot(oa_ref[...], wo_ref[:da, :], preferred_element_type=F32)
    acc = acc + jnp.dot(mixed, wo_ref[da:, :], preferred_element_type=F32)
    o_ref[...] = x_ref[...] + acc


def _pool_out_proj(proj, o_attn, x, pool_w, pool_scale, w_o):
    b, s, d_model = x.shape
    d_attn = o_attn.shape[2]
    c = d_model - d_attn
    tm, halo = POOL_TM, POOL_HALO
    assert s % tm == 0 and tm % halo == 0 and proj.shape[2] == 3 * d_attn + c
    assert (3 * d_attn) % c == 0
    u_blk = 3 * d_attn // c
    hb = tm // halo
    gd = c // len(POOL_WINDOWS)
    n = tm + halo
    return pl.pallas_call(
        _pool_out_kernel,
        grid=(b, s // tm),
        in_specs=[
            pl.BlockSpec((None, tm, c), lambda bi, i: (bi, i, u_blk)),
            pl.BlockSpec((None, halo, c),
                         lambda bi, i: (bi, jnp.maximum(i * hb - 1, 0), u_blk)),
            pl.BlockSpec((None, tm, d_attn), lambda bi, i: (bi, i, 0)),
            pl.BlockSpec((None, tm, d_model), lambda bi, i: (bi, i, 0)),
            _resident(pool_w.shape),
            _resident((1, c)),
            _resident(w_o.shape),
        ],
        out_specs=pl.BlockSpec((None, tm, d_model), lambda bi, i: (bi, i, 0)),
        out_shape=jax.ShapeDtypeStruct((b, s, d_model), F32),
        scratch_shapes=[
            pltpu.VMEM((n, c), F32),
            pltpu.VMEM((n, c), F32),
            pltpu.VMEM((n, c - gd), F32),
            pltpu.VMEM((n, c - 2 * gd), F32),
        ],
        compiler_params=_params(2),
        name="pool_out_proj",
    )(proj, proj, o_attn, x, pool_w, pool_scale, w_o)


def _cross_attn_kernel(x_ref, g_ref, wq_ref, kv_ref, wo_ref, o_ref, oh_ref):
    d_model = x_ref.shape[1]
    hd = d_model // N_XATTN_HEADS
    x = x_ref[...]
    hx = _rmsnorm_f32(x, g_ref[...]).astype(BF16)
    for h in range(N_XATTN_HEADS):
        q = jnp.dot(hx, wq_ref[:, h * hd:(h + 1) * hd], preferred_element_type=F32)
        q = (q * (hd ** -0.5 * LOG2E)).astype(BF16)
        kh = kv_ref[:, h * hd:(h + 1) * hd]
        vh = kv_ref[:, d_model + h * hd:d_model + (h + 1) * hd]
        s = lax.dot_general(q, kh, (((1,), (1,)), ((), ())),
                            preferred_element_type=F32)
        p = jnp.exp2(s - jnp.max(s, axis=1, keepdims=True))
        l = jnp.sum(p, axis=1, keepdims=True)
        oh = jnp.dot(p.astype(BF16), vh, preferred_element_type=F32) / l
        oh_ref[:, h * hd:(h + 1) * hd] = oh.astype(oh_ref.dtype)
    o_ref[...] = x + jnp.dot(oh_ref[...], wo_ref[...], preferred_element_type=F32)


def _cross_attention(x, g, wq, kv, wo):
    b, s, d_model = x.shape
    tm = XATT_TM
    assert s % tm == 0
    return pl.pallas_call(
        _cross_attn_kernel,
        grid=(b, s // tm),
        in_specs=[
            pl.BlockSpec((None, tm, d_model), lambda bi, i: (bi, i, 0)),
            _resident((1, d_model)),
            _resident(wq.shape),
            pl.BlockSpec((None,) + kv.shape[1:], lambda bi, i: (bi, 0, 0)),
            _resident(wo.shape),
        ],
        out_specs=pl.BlockSpec((None, tm, d_model), lambda bi, i: (bi, i, 0)),
        out_shape=jax.ShapeDtypeStruct((b, s, d_model), F32),
        scratch_shapes=[pltpu.VMEM((tm, d_model), BF16)],
        compiler_params=_params(2),
        name="cross_attention",
    )(x, g, wq, kv, wo)


def _ffn_kernel(x_ref, g_ref, wg_ref, wu_ref, wd_ref, gf_ref, o_ref, h_ref, *,
                final_norm):
    j = pl.program_id(1)
    last = pl.num_programs(1) - 1

    def glu_down(h):
        gate = jnp.dot(h, wg_ref[...], preferred_element_type=F32)
        up = jnp.dot(h, wu_ref[...], preferred_element_type=F32)
        act = (gate / (1.0 + jnp.exp(-gate)) * up).astype(BF16)
        return jnp.dot(act, wd_ref[...], preferred_element_type=F32)

    @pl.when(j == 0)
    def _():
        x = x_ref[...]
        h_ref[...] = _rmsnorm_f32(x, g_ref[...]).astype(h_ref.dtype)
        o_ref[...] = x + glu_down(h_ref[...])

    @pl.when(jnp.logical_and(j > 0, j < last))
    def _():
        o_ref[...] += glu_down(h_ref[...])

    @pl.when(j == last)
    def _():
        y = o_ref[...] + glu_down(h_ref[...])
        o_ref[...] = _rmsnorm_f32(y, gf_ref[...]) if final_norm else y


def _ffn(x, g, w_gate, w_up, w_down, g_final, *, final_norm):
    m, d_model = x.shape
    f = w_gate.shape[1]
    tm, tf = FFN_TM, FFN_TF
    assert m % tm == 0 and f % tf == 0 and f // tf >= 2
    return pl.pallas_call(
        functools.partial(_ffn_kernel, final_norm=final_norm),
        grid=(m // tm, f // tf),
        in_specs=[
            pl.BlockSpec((tm, d_model), lambda i, j: (i, 0)),
            pl.BlockSpec((1, d_model), lambda i, j: (0, 0)),
            pl.BlockSpec((d_model, tf), lambda i, j: (0, j)),
            pl.BlockSpec((d_model, tf), lambda i, j: (0, j)),
            pl.BlockSpec((tf, d_model), lambda i, j: (j, 0)),
            pl.BlockSpec((1, d_model), lambda i, j: (0, 0)),
        ],
        out_specs=pl.BlockSpec((tm, d_model), lambda i, j: (i, 0)),
        out_shape=jax.ShapeDtypeStruct((m, d_model), F32),
        scratch_shapes=[pltpu.VMEM((tm, d_model), BF16)],
        compiler_params=_params(2),
        name="swiglu_final_norm",
    )(x, g, w_gate, w_up, w_down, g_final)


def kernel(x, mem, norm_mix, w_in, lambda_q1, lambda_k1, lambda_q2, lambda_k2, subln, pool_w, pool_scale, w_o, norm_xattn, norm_mem, wq_x, wkv_x, wo_x, norm_ffn, w_gate, w_up, w_down, norm_final):
    b, s, d_model = x.shape
    depth = norm_mix.shape[0]
    d_attn = d_model // 2
    n_diff_heads = d_attn // (2 * DIFF_HEAD_DIM)
    d_in = w_in.shape[2]
    t = b * s

    col_scale = jnp.concatenate([
        jnp.full((1, d_attn), DIFF_HEAD_DIM ** -0.5 * LOG2E, F32),
        jnp.ones((1, d_in - d_attn), F32)], axis=1)
    ones_kv = jnp.ones((1, 2 * d_model), F32)
    row = lambda v: v.reshape(1, -1).astype(F32)

    for l in range(depth):
        lam_init = _lambda_init(l)
        proj = _norm_matmul(x.reshape(t, d_model), row(norm_mix[l]),
                            w_in[l].astype(BF16), col_scale,
                            tm=MM_TM, tn=MM_TN, name="in_proj")
        proj = proj.reshape(b, s, d_in)
        o_attn = _diff_attention(proj, row(lambda_q1[l]), row(lambda_k1[l]),
                                 row(lambda_q2[l]), row(lambda_k2[l]), row(subln[l]),
                                 n_heads=n_diff_heads, lam_init=lam_init)
        x = _pool_out_proj(proj, o_attn, x, pool_w[l].astype(BF16),
                           row(pool_scale[l]), w_o[l].astype(BF16))
        kv = _norm_matmul(mem.reshape(b * MEM_LEN, d_model), row(norm_mem[l]),
                          wkv_x[l].astype(BF16), ones_kv,
                          tm=b * MEM_LEN, tn=MM_TN, name="mem_kv_proj")
        x = _cross_attention(x, row(norm_xattn[l]), wq_x[l].astype(BF16),
                             kv.reshape(b, MEM_LEN, 2 * d_model), wo_x[l].astype(BF16))
        x = _ffn(x.reshape(t, d_model), row(norm_ffn[l]), w_gate[l].astype(BF16),
                 w_up[l].astype(BF16), w_down[l].astype(BF16), row(norm_final),
                 final_norm=(l == depth - 1)).reshape(b, s, d_model)
    return x
```

```python
import functools
import math

import jax
import jax.numpy as jnp
from jax import lax
from jax.experimental import pallas as pl
from jax.experimental.pallas import tpu as pltpu

MEM_LEN = 256
DIFF_HEAD_DIM = 128
POOL_WINDOWS = (2, 4, 8, 16)
N_XATTN_HEADS = 4
EPS = 1e-6
NEG_INF = -1e30
LOG2E = math.log2(math.e)

V7X_VMEM_BYTES = 64 * 1024 * 1024
V7X_LANES = 128
V7X_SUBLANES = 8
VMEM_LIMIT_BYTES = V7X_VMEM_BYTES - 8 * 1024 * 1024

MM_TM = 512
MM_TN = 1024
ATT_TQ = 1024
ATT_CK = 512
POOL_TM = 512
POOL_HALO = 32
XATT_TM = 512
FFN_TM = 1024
FFN_TF = 512

BF16 = jnp.bfloat16
F32 = jnp.float32


def _lambda_init(layer_idx):
    return 0.8 - 0.6 * math.exp(-0.3 * layer_idx)


def _rmsnorm_f32(x, g):
    y = x * lax.rsqrt(jnp.mean(x * x, axis=-1, keepdims=True) + EPS)
    return y * g


def _lane_tile(x, reps):
    return x if reps == 1 else jnp.concatenate([x] * reps, axis=1)


def _params(n_axes):
    return pltpu.CompilerParams(
        dimension_semantics=("arbitrary",) * n_axes,
        vmem_limit_bytes=VMEM_LIMIT_BYTES,
    )


def _resident(shape):
    nd = len(shape)
    return pl.BlockSpec(shape, lambda *_: (0,) * nd, pipeline_mode=pl.Buffered(1))


def _norm_matmul_kernel(x_ref, g_ref, w_ref, cs_ref, o_ref, *, tn):
    h = _rmsnorm_f32(x_ref[...], g_ref[...]).astype(BF16)
    for c0 in range(0, o_ref.shape[1], tn):
        acc = jnp.dot(h, w_ref[:, c0:c0 + tn], preferred_element_type=F32)
        o_ref[:, c0:c0 + tn] = (acc * cs_ref[:, c0:c0 + tn]).astype(o_ref.dtype)


def _norm_matmul(x, g, w, col_scale, *, tm, tn, name):
    m, k = x.shape
    n = w.shape[1]
    assert m % tm == 0 and n % tn == 0
    return pl.pallas_call(
        functools.partial(_norm_matmul_kernel, tn=tn),
        grid=(m // tm,),
        in_specs=[
            pl.BlockSpec((tm, k), lambda i: (i, 0)),
            _resident((1, k)),
            _resident((k, n)),
            _resident((1, n)),
        ],
        out_specs=pl.BlockSpec((tm, n), lambda i: (i, 0)),
        out_shape=jax.ShapeDtypeStruct((m, n), BF16),
        compiler_params=_params(1),
        name=name,
    )(x, g, w, col_scale)


def _diff_attn_kernel(q_ref, k_ref, v_ref, lq1_ref, lk1_ref, lq2_ref, lk2_ref,
                      sub_ref, *refs, lam_init, n_cast):
    w32_refs, o_ref, w16_refs = refs[:n_cast], refs[n_cast], refs[n_cast + 1:2 * n_cast + 1]
    m_ref, l_ref, acc_ref, p_ref, a_ref = refs[2 * n_cast + 1:]
    for w32, w16 in zip(w32_refs, w16_refs):
        w16[...] = w32[...].astype(w16.dtype)

    d = DIFF_HEAD_DIM
    tq = q_ref.shape[0]
    ck = ATT_CK
    n_band = tq // ck
    qi = pl.program_id(2)
    n_full = qi * n_band

    m_ref[...] = jnp.full(m_ref.shape, NEG_INF, F32)
    l_ref[...] = jnp.zeros(l_ref.shape, F32)
    acc_ref[...] = jnp.zeros(acc_ref.shape, F32)

    def probs(j, slot, row0, masked):
        rows = tq - row0
        kstart = pl.multiple_of(j * ck, ck)
        if masked:
            r_id = lax.broadcasted_iota(jnp.int32, (rows, ck), 0)
            c_id = lax.broadcasted_iota(jnp.int32, (rows, ck), 1)
            visible = c_id <= r_id
        for half in range(2):
            q = q_ref[row0:, half * d:(half + 1) * d]
            k = k_ref[pl.ds(kstart, ck), half * d:(half + 1) * d]
            s = lax.dot_general(q, k, (((1,), (1,)), ((), ())),
                                preferred_element_type=F32)
            if masked:
                s = jnp.where(visible, s, NEG_INF)
            m_prev = m_ref[half, row0:, :]
            m_new = jnp.maximum(m_prev, jnp.max(s, axis=1, keepdims=True))
            alpha = jnp.exp2(m_prev - m_new)
            p = jnp.exp2(s - _lane_tile(m_new, ck // V7X_LANES))
            l_ref[half, row0:, :] = (alpha * l_ref[half, row0:, :]
                                     + jnp.sum(p, axis=1, keepdims=True))
            m_ref[half, row0:, :] = m_new
            a_ref[slot, half, row0:, :] = alpha
            p_ref[slot, half, row0:, :] = p.astype(BF16)

    def values(j, slot, row0):
        kstart = pl.multiple_of(j * ck, ck)
        v = v_ref[pl.ds(kstart, ck), :]
        for half in range(2):
            pv = jnp.dot(p_ref[slot, half, row0:, :], v, preferred_element_type=F32)
            alpha = _lane_tile(a_ref[slot, half, row0:, :], 2 * d // V7X_LANES)
            acc_ref[half, row0:, :] = acc_ref[half, row0:, :] * alpha + pv

    for i in range(n_band):
        band = n_band - 1 - i
        probs(n_full + band, i % 2, band * ck, True)
        if i > 0:
            values(n_full + band + 1, (i - 1) % 2, (band + 1) * ck)

    def body(t, carry):
        j = 2 * t
        probs(j, 0, 0, False)
        values(jnp.where(t == 0, n_full, j - 1), 1, 0)
        probs(j + 1, 1, 0, False)
        values(j, 0, 0)
        return carry

    lax.fori_loop(0, qi * (n_band // 2), body, 0)
    values(jnp.where(n_full == 0, n_full, n_full - 1), 1, 0)

    lam = (jnp.exp(jnp.sum(lq1_ref[...] * lk1_ref[...], axis=1, keepdims=True))
           - jnp.exp(jnp.sum(lq2_ref[...] * lk2_ref[...], axis=1, keepdims=True))
           + lam_init)
    reps = 2 * d // V7X_LANES
    o1 = acc_ref[0] / _lane_tile(l_ref[0], reps)
    o2 = acc_ref[1] / _lane_tile(l_ref[1], reps)
    o = o1 - lam * o2
    o = _rmsnorm_f32(o, sub_ref[...]) * (1.0 - lam_init)
    o_ref[...] = o.astype(o_ref.dtype)


def _diff_attention(proj, lq1, lk1, lq2, lk2, subln, weights, *, n_heads, lam_init):
    b, s, _ = proj.shape
    hd = 2 * DIFF_HEAD_DIM
    tq = ATT_TQ
    assert s % tq == 0 and tq % (2 * ATT_CK) == 0
    nq = s // tq
    n_steps = b * n_heads * nq
    slab_rows = 2 * V7X_SUBLANES
    views = [w.reshape(n_steps * slab_rows, -1) for w in weights]
    assert all(v.shape[1] % V7X_LANES == 0 for v in views)
    slabs = [pl.BlockSpec((slab_rows, v.shape[1]),
                          lambda bi, h, i: ((bi * n_heads + h) * nq + i, 0))
             for v in views]
    vec = pl.BlockSpec((1, DIFF_HEAD_DIM), lambda bi, h, i: (0, 0))
    outs = pl.pallas_call(
        functools.partial(_diff_attn_kernel, lam_init=lam_init, n_cast=len(views)),
        grid=(b, n_heads, nq),
        in_specs=[
            pl.BlockSpec((None, tq, hd), lambda bi, h, i: (bi, i, h)),
            pl.BlockSpec((None, s, hd), lambda bi, h, i: (bi, 0, n_heads + h)),
            pl.BlockSpec((None, s, hd), lambda bi, h, i: (bi, 0, 2 * n_heads + h)),
            vec, vec, vec, vec,
            pl.BlockSpec((1, hd), lambda bi, h, i: (0, 0)),
        ] + slabs,
        out_specs=[pl.BlockSpec((None, tq, hd), lambda bi, h, i: (bi, i, h))] + slabs,
        out_shape=[jax.ShapeDtypeStruct((b, s, n_heads * hd), BF16)]
        + [jax.ShapeDtypeStruct(v.shape, BF16) for v in views],
        scratch_shapes=[
            pltpu.VMEM((2, tq, V7X_LANES), F32),
            pltpu.VMEM((2, tq, V7X_LANES), F32),
            pltpu.VMEM((2, tq, hd), F32),
            pltpu.VMEM((2, 2, tq, ATT_CK), BF16),
            pltpu.VMEM((2, 2, tq, V7X_LANES), F32),
        ],
        compiler_params=_params(3),
        name="diff_attention",
    )(proj, proj, proj, lq1, lk1, lq2, lk2, subln, *views)
    return outs[0], [o.reshape(w.shape) for o, w in zip(outs[1:], weights)]


def _pool_out_kernel(u_ref, uh_ref, oa_ref, x_ref, pw_ref, ps_ref, wo_ref, o_ref,
                     e_ref, a1_ref, a2_ref, a3_ref):
    tm, c = u_ref.shape
    halo = POOL_HALO
    n = tm + halo
    gd = c // len(POOL_WINDOWS)
    i = pl.program_id(1)

    u = u_ref[...].astype(F32)
    hist = uh_ref[...].astype(F32)
    e_ref[:halo, :] = jnp.where(i == 0, 0.0, hist)
    e_ref[halo:, :] = u
    a1_ref[8:, :] = e_ref[8:, :] + e_ref[7:n - 1, :]
    a2_ref[16:, :] = a1_ref[16:, gd:] + a1_ref[14:n - 2, gd:]
    a3_ref[24:, :] = a2_ref[24:, gd:] + a2_ref[20:n - 4, gd:]
    a4 = a3_ref[halo:, gd:] + a3_ref[halo - 8:n - 8, gd:]
    wsum = (a1_ref[halo:, :gd], a2_ref[halo:, :gd], a3_ref[halo:, :gd], a4)

    head = max(POOL_WINDOWS)
    pos1 = lax.broadcasted_iota(jnp.int32, (head, gd), 0) + (i * tm + 1)
    mixed = []
    for g, wl in enumerate(POOL_WINDOWS):
        ug = u[:, g * gd:(g + 1) * gd]
        inv_head = 1.0 / jnp.minimum(pos1, wl).astype(F32)
        pooled = jnp.concatenate([wsum[g][:head] * inv_head - ug[:head],
                                  wsum[g][head:] * (1.0 / wl) - ug[head:]], axis=0)
        mg = jnp.dot(pooled.astype(BF16), pw_ref[g], preferred_element_type=F32)
        mixed.append((mg * ps_ref[:, g * gd:(g + 1) * gd]).astype(BF16))
    mixed = jnp.concatenate(mixed, axis=1)

    da = oa_ref.shape[1]
    acc = jnp.dot(oa_ref[...], wo_ref[:da, :], preferred_element_type=F32)
    acc = acc + jnp.dot(mixed, wo_ref[da:, :], preferred_element_type=F32)
    o_ref[...] = x_ref[...] + acc


def _pool_out_proj(proj, o_attn, x, pool_w, pool_scale, w_o):
    b, s, d_model = x.shape
    d_attn = o_attn.shape[2]
    c = d_model - d_attn
    tm, halo = POOL_TM, POOL_HALO
    assert s % tm == 0 and tm % halo == 0 and proj.shape[2] == 3 * d_attn + c
    assert (3 * d_attn) % c == 0
    u_blk = 3 * d_attn // c
    hb = tm // halo
    gd = c // len(POOL_WINDOWS)
    n = tm + halo
    return pl.pallas_call(
        _pool_out_kernel,
        grid=(b, s // tm),
        in_specs=[
            pl.BlockSpec((None, tm, c), lambda bi, i: (bi, i, u_blk)),
            pl.BlockSpec((None, halo, c),
                         lambda bi, i: (bi, jnp.maximum(i * hb - 1, 0), u_blk)),
            pl.BlockSpec((None, tm, d_attn), lambda bi, i: (bi, i, 0)),
            pl.BlockSpec((None, tm, d_model), lambda bi, i: (bi, i, 0)),
            _resident(pool_w.shape),
            _resident((1, c)),
            _resident(w_o.shape),
        ],
        out_specs=pl.BlockSpec((None, tm, d_model), lambda bi, i: (bi, i, 0)),
        out_shape=jax.ShapeDtypeStruct((b, s, d_model), F32),
        scratch_shapes=[
            pltpu.VMEM((n, c), F32),
            pltpu.VMEM((n, c), F32),
            pltpu.VMEM((n, c - gd), F32),
            pltpu.VMEM((n, c - 2 * gd), F32),
        ],
        compiler_params=_params(2),
        name="pool_out_proj",
    )(proj, proj, o_attn, x, pool_w, pool_scale, w_o)


def _cross_attn_kernel(x_ref, g_ref, wq_ref, kv_ref, wo_ref, o_ref, oh_ref):
    d_model = x_ref.shape[1]
    hd = d_model // N_XATTN_HEADS
    x = x_ref[...]
    hx = _rmsnorm_f32(x, g_ref[...]).astype(BF16)
    for h in range(N_XATTN_HEADS):
        q = jnp.dot(hx, wq_ref[:, h * hd:(h + 1) * hd], preferred_element_type=F32)
        q = (q * (hd ** -0.5 * LOG2E)).astype(BF16)
        kh = kv_ref[:, h * hd:(h + 1) * hd]
        vh = kv_ref[:, d_model + h * hd:d_model + (h + 1) * hd]
        s = lax.dot_general(q, kh, (((1,), (1,)), ((), ())),
                            preferred_element_type=F32)
        p = jnp.exp2(s - jnp.max(s, axis=1, keepdims=True))
        l = jnp.sum(p, axis=1, keepdims=True)
        oh = jnp.dot(p.astype(BF16), vh, preferred_element_type=F32) / l
        oh_ref[:, h * hd:(h + 1) * hd] = oh.astype(oh_ref.dtype)
    o_ref[...] = x + jnp.dot(oh_ref[...], wo_ref[...], preferred_element_type=F32)


def _cross_attention(x, g, wq, kv, wo):
    b, s, d_model = x.shape
    tm = XATT_TM
    assert s % tm == 0
    return pl.pallas_call(
        _cross_attn_kernel,
        grid=(b, s // tm),
        in_specs=[
            pl.BlockSpec((None, tm, d_model), lambda bi, i: (bi, i, 0)),
            _resident((1, d_model)),
            _resident(wq.shape),
            pl.BlockSpec((None,) + kv.shape[1:], lambda bi, i: (bi, 0, 0)),
            _resident(wo.shape),
        ],
        out_specs=pl.BlockSpec((None, tm, d_model), lambda bi, i: (bi, i, 0)),
        out_shape=jax.ShapeDtypeStruct((b, s, d_model), F32),
        scratch_shapes=[pltpu.VMEM((tm, d_model), BF16)],
        compiler_params=_params(2),
        name="cross_attention",
    )(x, g, wq, kv, wo)


def _ffn_kernel(x_ref, g_ref, wg_ref, wu_ref, wd_ref, gf_ref, o_ref, h_ref, *,
                final_norm):
    j = pl.program_id(1)
    last = pl.num_programs(1) - 1

    def glu_down(h):
        gate = jnp.dot(h, wg_ref[...], preferred_element_type=F32)
        up = jnp.dot(h, wu_ref[...], preferred_element_type=F32)
        act = (gate / (1.0 + jnp.exp(-gate)) * up).astype(BF16)
        return jnp.dot(act, wd_ref[...], preferred_element_type=F32)

    @pl.when(j == 0)
    def _():
        x = x_ref[...]
        h_ref[...] = _rmsnorm_f32(x, g_ref[...]).astype(h_ref.dtype)
        o_ref[...] = x + glu_down(h_ref[...])

    @pl.when(jnp.logical_and(j > 0, j < last))
    def _():
        o_ref[...] += glu_down(h_ref[...])

    @pl.when(j == last)
    def _():
        y = o_ref[...] + glu_down(h_ref[...])
        o_ref[...] = _rmsnorm_f32(y, gf_ref[...]) if final_norm else y


def _ffn(x, g, w_gate, w_up, w_down, g_final, *, final_norm):
    m, d_model = x.shape
    f = w_gate.shape[1]
    tm, tf = FFN_TM, FFN_TF
    assert m % tm == 0 and f % tf == 0 and f // tf >= 2
    return pl.pallas_call(
        functools.partial(_ffn_kernel, final_norm=final_norm),
        grid=(m // tm, f // tf),
        in_specs=[
            pl.BlockSpec((tm, d_model), lambda i, j: (i, 0)),
            pl.BlockSpec((1, d_model), lambda i, j: (0, 0)),
            pl.BlockSpec((d_model, tf), lambda i, j: (0, j)),
            pl.BlockSpec((d_model, tf), lambda i, j: (0, j)),
            pl.BlockSpec((tf, d_model), lambda i, j: (j, 0)),
            pl.BlockSpec((1, d_model), lambda i, j: (0, 0)),
        ],
        out_specs=pl.BlockSpec((tm, d_model), lambda i, j: (i, 0)),
        out_shape=jax.ShapeDtypeStruct((m, d_model), F32),
        scratch_shapes=[pltpu.VMEM((tm, d_model), BF16)],
        compiler_params=_params(2),
        name="swiglu_final_norm",
    )(x, g, w_gate, w_up, w_down, g_final)


def kernel(x, mem, norm_mix, w_in, lambda_q1, lambda_k1, lambda_q2, lambda_k2, subln, pool_w, pool_scale, w_o, norm_xattn, norm_mem, wq_x, wkv_x, wo_x, norm_ffn, w_gate, w_up, w_down, norm_final):
    b, s, d_model = x.shape
    depth = norm_mix.shape[0]
    d_attn = d_model // 2
    n_diff_heads = d_attn // (2 * DIFF_HEAD_DIM)
    d_in = w_in.shape[2]
    t = b * s

    col_scale = jnp.concatenate([
        jnp.full((1, d_attn), DIFF_HEAD_DIM ** -0.5 * LOG2E, F32),
        jnp.ones((1, d_in - d_attn), F32)], axis=1)
    ones_kv = jnp.ones((1, 2 * d_model), F32)
    row = lambda v: v.reshape(1, -1).astype(F32)

    for l in range(depth):
        lam_init = _lambda_init(l)
        proj = _norm_matmul(x.reshape(t, d_model), row(norm_mix[l]),
                            w_in[l].astype(BF16), col_scale,
                            tm=MM_TM, tn=MM_TN, name="in_proj")
        proj = proj.reshape(b, s, d_in)
        later = (pool_w[l], w_o[l], wkv_x[l], wq_x[l], wo_x[l], w_gate[l], w_up[l],
                 w_down[l])
        o_attn, (pool_w16, w_o16, wkv16, wq16, wo16, wg16, wu16, wd16) = _diff_attention(
            proj, row(lambda_q1[l]), row(lambda_k1[l]), row(lambda_q2[l]),
            row(lambda_k2[l]), row(subln[l]), later,
            n_heads=n_diff_heads, lam_init=lam_init)
        x = _pool_out_proj(proj, o_attn, x, pool_w16, row(pool_scale[l]), w_o16)
        kv = _norm_matmul(mem.reshape(b * MEM_LEN, d_model), row(norm_mem[l]),
                          wkv16, ones_kv, tm=b * MEM_LEN, tn=MM_TN, name="mem_kv_proj")
        x = _cross_attention(x, row(norm_xattn[l]), wq16,
                             kv.reshape(b, MEM_LEN, 2 * d_model), wo16)
        x = _ffn(x.reshape(t, d_model), row(norm_ffn[l]), wg16, wu16, wd16,
                 row(norm_final), final_norm=(l == depth - 1)).reshape(b, s, d_model)
    return x
```

```python
import functools
import math

import jax
import jax.numpy as jnp
from jax import lax
from jax.experimental import pallas as pl
from jax.experimental.pallas import tpu as pltpu

MEM_LEN = 256
DIFF_HEAD_DIM = 128
POOL_WINDOWS = (2, 4, 8, 16)
N_XATTN_HEADS = 4
EPS = 1e-6
NEG_INF = -1e30
LOG2E = math.log2(math.e)

V7X_VMEM_BYTES = 64 * 1024 * 1024
V7X_LANES = 128
V7X_SUBLANES = 8
VMEM_LIMIT_BYTES = V7X_VMEM_BYTES - 8 * 1024 * 1024

MM_TM = 512
MM_TN = 1024
ATT_TQ = 1024
ATT_CK = 512
POOL_TM = 512
POOL_HALO = 32
XATT_TM = 512
FFN_TM = 1024
FFN_TF = 512

BF16 = jnp.bfloat16
F32 = jnp.float32


def _lambda_init(layer_idx):
    return 0.8 - 0.6 * math.exp(-0.3 * layer_idx)


def _rmsnorm_f32(x, g):
    y = x * lax.rsqrt(jnp.mean(x * x, axis=-1, keepdims=True) + EPS)
    return y * g


def _lane_tile(x, reps):
    return x if reps == 1 else jnp.concatenate([x] * reps, axis=1)


def _params(n_axes):
    return pltpu.CompilerParams(
        dimension_semantics=("arbitrary",) * n_axes,
        vmem_limit_bytes=VMEM_LIMIT_BYTES,
    )


def _resident(shape):
    nd = len(shape)
    return pl.BlockSpec(shape, lambda *_: (0,) * nd, pipeline_mode=pl.Buffered(1))


def _norm_matmul_kernel(x_ref, g_ref, w_ref, cs_ref, o_ref, *, tn):
    h = _rmsnorm_f32(x_ref[...], g_ref[...]).astype(BF16)
    for c0 in range(0, o_ref.shape[1], tn):
        acc = jnp.dot(h, w_ref[:, c0:c0 + tn], preferred_element_type=F32)
        o_ref[:, c0:c0 + tn] = (acc * cs_ref[:, c0:c0 + tn]).astype(o_ref.dtype)


def _norm_matmul(x, g, w, col_scale, *, tm, tn, name):
    m, k = x.shape
    n = w.shape[1]
    assert m % tm == 0 and n % tn == 0
    return pl.pallas_call(
        functools.partial(_norm_matmul_kernel, tn=tn),
        grid=(m // tm,),
        in_specs=[
            pl.BlockSpec((tm, k), lambda i: (i, 0)),
            _resident((1, k)),
            _resident((k, n)),
            _resident((1, n)),
        ],
        out_specs=pl.BlockSpec((tm, n), lambda i: (i, 0)),
        out_shape=jax.ShapeDtypeStruct((m, n), BF16),
        compiler_params=_params(1),
        name=name,
    )(x, g, w, col_scale)


def _diff_attn_kernel(q_ref, k_ref, v_ref, lq1_ref, lk1_ref, lq2_ref, lk2_ref,
                      sub_ref, *refs, lam_init, n_cast):
    w32_refs, o_ref, w16_refs = refs[:n_cast], refs[n_cast], refs[n_cast + 1:2 * n_cast + 1]
    m_ref, l_ref, acc_ref, p_ref, a_ref = refs[2 * n_cast + 1:]
    for w32, w16 in zip(w32_refs, w16_refs):
        w16[...] = w32[...].astype(w16.dtype)

    d = DIFF_HEAD_DIM
    tq = q_ref.shape[0]
    ck = ATT_CK
    n_band = tq // ck
    qi = pl.program_id(2)
    n_full = qi * n_band

    m_ref[...] = jnp.full(m_ref.shape, NEG_INF, F32)
    l_ref[...] = jnp.zeros(l_ref.shape, F32)
    acc_ref[...] = jnp.zeros(acc_ref.shape, F32)

    def probs(j, slot, row0, masked):
        rows = tq - row0
        kstart = pl.multiple_of(j * ck, ck)
        if masked:
            r_id = lax.broadcasted_iota(jnp.int32, (rows, ck), 0)
            c_id = lax.broadcasted_iota(jnp.int32, (rows, ck), 1)
            visible = c_id <= r_id
        for half in range(2):
            q = q_ref[row0:, half * d:(half + 1) * d]
            k = k_ref[pl.ds(kstart, ck), half * d:(half + 1) * d]
            s = lax.dot_general(q, k, (((1,), (1,)), ((), ())),
                                preferred_element_type=F32)
            if masked:
                s = jnp.where(visible, s, NEG_INF)
            m_prev = m_ref[half, row0:, :]
            m_new = jnp.maximum(m_prev, jnp.max(s, axis=1, keepdims=True))
            alpha = jnp.exp2(m_prev - m_new)
            p = jnp.exp2(s - _lane_tile(m_new, ck // V7X_LANES))
            l_ref[half, row0:, :] = (alpha * l_ref[half, row0:, :]
                                     + jnp.sum(p, axis=1, keepdims=True))
            m_ref[half, row0:, :] = m_new
            a_ref[slot, half, row0:, :] = alpha
            p_ref[slot, half, row0:, :] = p.astype(BF16)

    def values(j, slot, row0):
        kstart = pl.multiple_of(j * ck, ck)
        v = v_ref[pl.ds(kstart, ck), :]
        for half in range(2):
            pv = jnp.dot(p_ref[slot, half, row0:, :], v, preferred_element_type=F32)
            alpha = _lane_tile(a_ref[slot, half, row0:, :], 2 * d // V7X_LANES)
            acc_ref[half, row0:, :] = acc_ref[half, row0:, :] * alpha + pv

    for i in range(n_band):
        band = n_band - 1 - i
        probs(n_full + band, i % 2, band * ck, True)
        if i > 0:
            values(n_full + band + 1, (i - 1) % 2, (band + 1) * ck)

    def body(t, carry):
        j = 2 * t
        probs(j, 0, 0, False)
        values(jnp.where(t == 0, n_full, j - 1), 1, 0)
        probs(j + 1, 1, 0, False)
        values(j, 0, 0)
        return carry

    lax.fori_loop(0, qi * (n_band // 2), body, 0)
    values(jnp.where(n_full == 0, n_full, n_full - 1), 1, 0)

    lam = (jnp.exp(jnp.sum(lq1_ref[...] * lk1_ref[...], axis=1, keepdims=True))
           - jnp.exp(jnp.sum(lq2_ref[...] * lk2_ref[...], axis=1, keepdims=True))
           + lam_init)
    reps = 2 * d // V7X_LANES
    o1 = acc_ref[0] / _lane_tile(l_ref[0], reps)
    o2 = acc_ref[1] / _lane_tile(l_ref[1], reps)
    o = o1 - lam * o2
    o = _rmsnorm_f32(o, sub_ref[...]) * (1.0 - lam_init)
    o_ref[...] = o.astype(o_ref.dtype)


def _diff_attention(proj, lq1, lk1, lq2, lk2, subln, weights, layer, *, n_heads,
                    lam_init):
    b, s, _ = proj.shape
    hd = 2 * DIFF_HEAD_DIM
    tq = ATT_TQ
    assert s % tq == 0 and tq % (2 * ATT_CK) == 0
    nq = s // tq
    n_steps = b * n_heads * nq
    bf16_rows = 2 * V7X_SUBLANES

    def slab_specs(w):
        _, r, c = w.shape
        group = next(g for g in range(1, n_steps + 1)
                     if n_steps % g == 0 and r % (n_steps // g) == 0
                     and (r // (n_steps // g)) % bf16_rows == 0)
        rs = r // (n_steps // group)
        step = lambda bi, h, i: ((bi * n_heads + h) * nq + i) // group
        return (pl.BlockSpec((None, rs, c), lambda bi, h, i: (layer, step(bi, h, i), 0)),
                pl.BlockSpec((rs, c), lambda bi, h, i: (step(bi, h, i), 0)))

    specs = [slab_specs(w) for w in weights]
    in_slabs = [sp[0] for sp in specs]
    out_slabs = [sp[1] for sp in specs]
    vec = pl.BlockSpec((1, DIFF_HEAD_DIM), lambda bi, h, i: (0, 0))
    outs = pl.pallas_call(
        functools.partial(_diff_attn_kernel, lam_init=lam_init, n_cast=len(weights)),
        grid=(b, n_heads, nq),
        in_specs=[
            pl.BlockSpec((None, tq, hd), lambda bi, h, i: (bi, i, h)),
            pl.BlockSpec((None, s, hd), lambda bi, h, i: (bi, 0, n_heads + h)),
            pl.BlockSpec((None, s, hd), lambda bi, h, i: (bi, 0, 2 * n_heads + h)),
            vec, vec, vec, vec,
            pl.BlockSpec((1, hd), lambda bi, h, i: (0, 0)),
        ] + in_slabs,
        out_specs=[pl.BlockSpec((None, tq, hd), lambda bi, h, i: (bi, i, h))]
        + out_slabs,
        out_shape=[jax.ShapeDtypeStruct((b, s, n_heads * hd), BF16)]
        + [jax.ShapeDtypeStruct(w.shape[1:], BF16) for w in weights],
        scratch_shapes=[
            pltpu.VMEM((2, tq, V7X_LANES), F32),
            pltpu.VMEM((2, tq, V7X_LANES), F32),
            pltpu.VMEM((2, tq, hd), F32),
            pltpu.VMEM((2, 2, tq, ATT_CK), BF16),
            pltpu.VMEM((2, 2, tq, V7X_LANES), F32),
        ],
        compiler_params=_params(3),
        name="diff_attention",
    )(proj, proj, proj, lq1, lk1, lq2, lk2, subln, *weights)
    return outs[0], outs[1:]


def _pool_out_kernel(u_ref, uh_ref, oa_ref, x_ref, pw_ref, ps_ref, wo_ref, o_ref,
                     e_ref, a1_ref, a2_ref, a3_ref):
    tm, c = u_ref.shape
    halo = POOL_HALO
    n = tm + halo
    gd = c // len(POOL_WINDOWS)
    i = pl.program_id(1)

    u = u_ref[...].astype(F32)
    hist = uh_ref[...].astype(F32)
    e_ref[:halo, :] = jnp.where(i == 0, 0.0, hist)
    e_ref[halo:, :] = u
    a1_ref[8:, :] = e_ref[8:, :] + e_ref[7:n - 1, :]
    a2_ref[16:, :] = a1_ref[16:, gd:] + a1_ref[14:n - 2, gd:]
    a3_ref[24:, :] = a2_ref[24:, gd:] + a2_ref[20:n - 4, gd:]
    a4 = a3_ref[halo:, gd:] + a3_ref[halo - 8:n - 8, gd:]
    wsum = (a1_ref[halo:, :gd], a2_ref[halo:, :gd], a3_ref[halo:, :gd], a4)

    head = max(POOL_WINDOWS)
    pos1 = lax.broadcasted_iota(jnp.int32, (head, gd), 0) + (i * tm + 1)
    mixed = []
    for g, wl in enumerate(POOL_WINDOWS):
        ug = u[:, g * gd:(g + 1) * gd]
        inv_head = 1.0 / jnp.minimum(pos1, wl).astype(F32)
        pooled = jnp.concatenate([wsum[g][:head] * inv_head - ug[:head],
                                  wsum[g][head:] * (1.0 / wl) - ug[head:]], axis=0)
        mg = jnp.dot(pooled.astype(BF16), pw_ref[g], preferred_element_type=F32)
        mixed.append((mg * ps_ref[:, g * gd:(g + 1) * gd]).astype(BF16))
    mixed = jnp.concatenate(mixed, axis=1)

    da = oa_ref.shape[1]
    acc = jnp.dot(oa_ref[...], wo_ref[:da, :], preferred_element_type=F32)
    acc = acc + jnp.dot(mixed, wo_ref[da:, :], preferred_element_type=F32)
    o_ref[...] = x_ref[...] + acc


def _pool_out_proj(proj, o_attn, x, pool_w, pool_scale, w_o):
    b, s, d_model = x.shape
    d_attn = o_attn.shape[2]
    c = d_model - d_attn
    tm, halo = POOL_TM, POOL_HALO
    assert s % tm == 0 and tm % halo == 0 and proj.shape[2] == 3 * d_attn + c
    assert (3 * d_attn) % c == 0
    u_blk = 3 * d_attn // c
    hb = tm // halo
    gd = c // len(POOL_WINDOWS)
    n = tm + halo
    return pl.pallas_call(
        _pool_out_kernel,
        grid=(b, s // tm),
        in_specs=[
            pl.BlockSpec((None, tm, c), lambda bi, i: (bi, i, u_blk)),
            pl.BlockSpec((None, halo, c),
                         lambda bi, i: (bi, jnp.maximum(i * hb - 1, 0), u_blk)),
            pl.BlockSpec((None, tm, d_attn), lambda bi, i: (bi, i, 0)),
            pl.BlockSpec((None, tm, d_model), lambda bi, i: (bi, i, 0)),
            _resident(pool_w.shape),
            _resident((1, c)),
            _resident(w_o.shape),
        ],
        out_specs=pl.BlockSpec((None, tm, d_model), lambda bi, i: (bi, i, 0)),
        out_shape=jax.ShapeDtypeStruct((b, s, d_model), F32),
        scratch_shapes=[
            pltpu.VMEM((n, c), F32),
            pltpu.VMEM((n, c), F32),
            pltpu.VMEM((n, c - gd), F32),
            pltpu.VMEM((n, c - 2 * gd), F32),
        ],
        compiler_params=_params(2),
        name="pool_out_proj",
    )(proj, proj, o_attn, x, pool_w, pool_scale, w_o)


def _cross_attn_kernel(x_ref, g_ref, wq_ref, kv_ref, wo_ref, o_ref, oh_ref):
    d_model = x_ref.shape[1]
    hd = d_model // N_XATTN_HEADS
    x = x_ref[...]
    hx = _rmsnorm_f32(x, g_ref[...]).astype(BF16)
    for h in range(N_XATTN_HEADS):
        q = jnp.dot(hx, wq_ref[:, h * hd:(h + 1) * hd], preferred_element_type=F32)
        q = (q * (hd ** -0.5 * LOG2E)).astype(BF16)
        kh = kv_ref[:, h * hd:(h + 1) * hd]
        vh = kv_ref[:, d_model + h * hd:d_model + (h + 1) * hd]
        s = lax.dot_general(q, kh, (((1,), (1,)), ((), ())),
                            preferred_element_type=F32)
        p = jnp.exp2(s - jnp.max(s, axis=1, keepdims=True))
        l = jnp.sum(p, axis=1, keepdims=True)
        oh = jnp.dot(p.astype(BF16), vh, preferred_element_type=F32) / l
        oh_ref[:, h * hd:(h + 1) * hd] = oh.astype(oh_ref.dtype)
    o_ref[...] = x + jnp.dot(oh_ref[...], wo_ref[...], preferred_element_type=F32)


def _cross_attention(x, g, wq, kv, wo):
    b, s, d_model = x.shape
    tm = XATT_TM
    assert s % tm == 0
    return pl.pallas_call(
        _cross_attn_kernel,
        grid=(b, s // tm),
        in_specs=[
            pl.BlockSpec((None, tm, d_model), lambda bi, i: (bi, i, 0)),
            _resident((1, d_model)),
            _resident(wq.shape),
            pl.BlockSpec((None,) + kv.shape[1:], lambda bi, i: (bi, 0, 0)),
            _resident(wo.shape),
        ],
        out_specs=pl.BlockSpec((None, tm, d_model), lambda bi, i: (bi, i, 0)),
        out_shape=jax.ShapeDtypeStruct((b, s, d_model), F32),
        scratch_shapes=[pltpu.VMEM((tm, d_model), BF16)],
        compiler_params=_params(2),
        name="cross_attention",
    )(x, g, wq, kv, wo)


def _ffn_kernel(x_ref, g_ref, wg_ref, wu_ref, wd_ref, gf_ref, o_ref, h_ref, *,
                final_norm):
    j = pl.program_id(1)
    last = pl.num_programs(1) - 1

    def glu_down(h):
        gate = jnp.dot(h, wg_ref[...], preferred_element_type=F32)
        up = jnp.dot(h, wu_ref[...], preferred_element_type=F32)
        act = (gate / (1.0 + jnp.exp(-gate)) * up).astype(BF16)
        return jnp.dot(act, wd_ref[...], preferred_element_type=F32)

    @pl.when(j == 0)
    def _():
        x = x_ref[...]
        h_ref[...] = _rmsnorm_f32(x, g_ref[...]).astype(h_ref.dtype)
        o_ref[...] = x + glu_down(h_ref[...])

    @pl.when(jnp.logical_and(j > 0, j < last))
    def _():
        o_ref[...] += glu_down(h_ref[...])

    @pl.when(j == last)
    def _():
        y = o_ref[...] + glu_down(h_ref[...])
        o_ref[...] = _rmsnorm_f32(y, gf_ref[...]) if final_norm else y


def _ffn(x, g, w_gate, w_up, w_down, g_final, *, final_norm):
    m, d_model = x.shape
    f = w_gate.shape[1]
    tm, tf = FFN_TM, FFN_TF
    assert m % tm == 0 and f % tf == 0 and f // tf >= 2
    return pl.pallas_call(
        functools.partial(_ffn_kernel, final_norm=final_norm),
        grid=(m // tm, f // tf),
        in_specs=[
            pl.BlockSpec((tm, d_model), lambda i, j: (i, 0)),
            pl.BlockSpec((1, d_model), lambda i, j: (0, 0)),
            pl.BlockSpec((d_model, tf), lambda i, j: (0, j)),
            pl.BlockSpec((d_model, tf), lambda i, j: (0, j)),
            pl.BlockSpec((tf, d_model), lambda i, j: (j, 0)),
            pl.BlockSpec((1, d_model), lambda i, j: (0, 0)),
        ],
        out_specs=pl.BlockSpec((tm, d_model), lambda i, j: (i, 0)),
        out_shape=jax.ShapeDtypeStruct((m, d_model), F32),
        scratch_shapes=[pltpu.VMEM((tm, d_model), BF16)],
        compiler_params=_params(2),
        name="swiglu_final_norm",
    )(x, g, w_gate, w_up, w_down, g_final)


def kernel(x, mem, norm_mix, w_in, lambda_q1, lambda_k1, lambda_q2, lambda_k2, subln, pool_w, pool_scale, w_o, norm_xattn, norm_mem, wq_x, wkv_x, wo_x, norm_ffn, w_gate, w_up, w_down, norm_final):
    b, s, d_model = x.shape
    depth = norm_mix.shape[0]
    d_attn = d_model // 2
    n_diff_heads = d_attn // (2 * DIFF_HEAD_DIM)
    d_in = w_in.shape[2]
    t = b * s

    col_scale = jnp.concatenate([
        jnp.full((1, d_attn), DIFF_HEAD_DIM ** -0.5 * LOG2E, F32),
        jnp.ones((1, d_in - d_attn), F32)], axis=1)
    ones_kv = jnp.ones((1, 2 * d_model), F32)
    row = lambda v: v.reshape(1, -1).astype(F32)

    for l in range(depth):
        lam_init = _lambda_init(l)
        proj = _norm_matmul(x.reshape(t, d_model), row(norm_mix[l]),
                            w_in[l].astype(BF16), col_scale,
                            tm=MM_TM, tn=MM_TN, name="in_proj")
        proj = proj.reshape(b, s, d_in)
        later = (w_o, wkv_x, wq_x, wo_x, w_gate, w_up, w_down)
        o_attn, (w_o16, wkv16, wq16, wo16, wg16, wu16, wd16) = _diff_attention(
            proj, row(lambda_q1[l]), row(lambda_k1[l]), row(lambda_q2[l]),
            row(lambda_k2[l]), row(subln[l]), later, l,
            n_heads=n_diff_heads, lam_init=lam_init)
        pool_w16 = pool_w[l].astype(BF16)
        x = _pool_out_proj(proj, o_attn, x, pool_w16, row(pool_scale[l]), w_o16)
        kv = _norm_matmul(mem.reshape(b * MEM_LEN, d_model), row(norm_mem[l]),
                          wkv16, ones_kv, tm=b * MEM_LEN, tn=MM_TN, name="mem_kv_proj")
        x = _cross_attention(x, row(norm_xattn[l]), wq16,
                             kv.reshape(b, MEM_LEN, 2 * d_model), wo16)
        x = _ffn(x.reshape(t, d_model), row(norm_ffn[l]), wg16, wu16, wd16,
                 row(norm_final), final_norm=(l == depth - 1)).reshape(b, s, d_model)
    return x
```

```python
import functools
import math

import jax
import jax.numpy as jnp
from jax import lax
from jax.experimental import pallas as pl
from jax.experimental.pallas import tpu as pltpu

MEM_LEN = 256
DIFF_HEAD_DIM = 128
POOL_WINDOWS = (2, 4, 8, 16)
N_XATTN_HEADS = 4
EPS = 1e-6
NEG_INF = -1e30
LOG2E = math.log2(math.e)

V7X_VMEM_BYTES = 64 * 1024 * 1024
V7X_LANES = 128
V7X_SUBLANES = 8
VMEM_LIMIT_BYTES = V7X_VMEM_BYTES - 8 * 1024 * 1024

MM_TM = 512
MM_TN = 1024
ATT_TQ = 1024
ATT_CK = 512
POOL_TM = 512
POOL_HALO = 32
XATT_TM = 512
FFN_TM = 1024
FFN_TF = 512

BF16 = jnp.bfloat16
F32 = jnp.float32


def _lambda_init(layer_idx):
    return 0.8 - 0.6 * math.exp(-0.3 * layer_idx)


def _rmsnorm_f32(x, g):
    y = x * lax.rsqrt(jnp.mean(x * x, axis=-1, keepdims=True) + EPS)
    return y * g


def _lane_tile(x, reps):
    return x if reps == 1 else jnp.concatenate([x] * reps, axis=1)


def _params(n_axes):
    return pltpu.CompilerParams(
        dimension_semantics=("arbitrary",) * n_axes,
        vmem_limit_bytes=VMEM_LIMIT_BYTES,
    )


def _resident(shape):
    nd = len(shape)
    return pl.BlockSpec(shape, lambda *_: (0,) * nd, pipeline_mode=pl.Buffered(1))


def _norm_matmul_kernel(x_ref, g_ref, w_ref, cs_ref, o_ref, *, tn):
    h = _rmsnorm_f32(x_ref[...], g_ref[...]).astype(BF16)
    for c0 in range(0, o_ref.shape[1], tn):
        acc = jnp.dot(h, w_ref[:, c0:c0 + tn], preferred_element_type=F32)
        o_ref[:, c0:c0 + tn] = (acc * cs_ref[:, c0:c0 + tn]).astype(o_ref.dtype)


def _norm_matmul(x, g, w, col_scale, *, tm, tn, name):
    m, k = x.shape
    n = w.shape[1]
    assert m % tm == 0 and n % tn == 0
    return pl.pallas_call(
        functools.partial(_norm_matmul_kernel, tn=tn),
        grid=(m // tm,),
        in_specs=[
            pl.BlockSpec((tm, k), lambda i: (i, 0)),
            _resident((1, k)),
            _resident((k, n)),
            _resident((1, n)),
        ],
        out_specs=pl.BlockSpec((tm, n), lambda i: (i, 0)),
        out_shape=jax.ShapeDtypeStruct((m, n), BF16),
        compiler_params=_params(1),
        name=name,
    )(x, g, w, col_scale)


def _diff_attn_kernel(q_ref, k_ref, v_ref, lq1_ref, lk1_ref, lq2_ref, lk2_ref,
                      sub_ref, *refs, lam_init, n_cast):
    w32_refs, o_ref, w16_refs = refs[:n_cast], refs[n_cast], refs[n_cast + 1:2 * n_cast + 1]
    m_ref, l_ref, acc_ref, p_ref, a_ref = refs[2 * n_cast + 1:]
    for w32, w16 in zip(w32_refs, w16_refs):
        w16[...] = w32[...].astype(w16.dtype)

    d = DIFF_HEAD_DIM
    tq = q_ref.shape[0]
    ck = ATT_CK
    n_band = tq // ck
    qi = pl.program_id(2)
    n_full = qi * n_band

    m_ref[...] = jnp.full(m_ref.shape, NEG_INF, F32)
    l_ref[...] = jnp.zeros(l_ref.shape, F32)
    acc_ref[...] = jnp.zeros(acc_ref.shape, F32)

    def probs(j, slot, row0, row1, masked):
        rows = row1 - row0
        kstart = pl.multiple_of(j * ck, ck)
        if masked:
            r_id = lax.broadcasted_iota(jnp.int32, (rows, ck), 0)
            c_id = lax.broadcasted_iota(jnp.int32, (rows, ck), 1)
            visible = c_id <= r_id
        for half in range(2):
            q = q_ref[row0:row1, half * d:(half + 1) * d]
            k = k_ref[pl.ds(kstart, ck), half * d:(half + 1) * d]
            s = lax.dot_general(q, k, (((1,), (1,)), ((), ())),
                                preferred_element_type=F32)
            if masked:
                s = jnp.where(visible, s, NEG_INF)
            m_prev = m_ref[half, row0:row1, :]
            m_new = jnp.maximum(m_prev, jnp.max(s, axis=1, keepdims=True))
            alpha = jnp.exp2(m_prev - m_new)
            p = jnp.exp2(s - _lane_tile(m_new, ck // V7X_LANES))
            l_ref[half, row0:row1, :] = (alpha * l_ref[half, row0:row1, :]
                                         + jnp.sum(p, axis=1, keepdims=True))
            m_ref[half, row0:row1, :] = m_new
            a_ref[slot, half, row0:row1, :] = alpha
            p_ref[slot, half, row0:row1, :] = p.astype(BF16)

    def values(j, slot, row0, row1):
        kstart = pl.multiple_of(j * ck, ck)
        v = v_ref[pl.ds(kstart, ck), :]
        for half in range(2):
            pv = jnp.dot(p_ref[slot, half, row0:row1, :], v, preferred_element_type=F32)
            alpha = _lane_tile(a_ref[slot, half, row0:row1, :], 2 * d // V7X_LANES)
            acc_ref[half, row0:row1, :] = acc_ref[half, row0:row1, :] * alpha + pv

    for i in range(n_band):
        band = n_band - 1 - i
        for r in range(band, n_band):
            probs(n_full + band, i % 2, r * ck, (r + 1) * ck, r == band)
            if i > 0 and r > band:
                values(n_full + band + 1, (i - 1) % 2, r * ck, (r + 1) * ck)

    def pair_step(j_new, slot_new, j_old):
        for r0 in range(0, tq, ck):
            probs(j_new, slot_new, r0, r0 + ck, False)
            values(j_old, 1 - slot_new, r0, r0 + ck)

    def body(t, carry):
        j = 2 * t
        pair_step(j, 0, jnp.where(t == 0, n_full, j - 1))
        pair_step(j + 1, 1, j)
        return carry

    lax.fori_loop(0, qi * (n_band // 2), body, 0)
    values(jnp.where(n_full == 0, n_full, n_full - 1), 1, 0, tq)

    lam = (jnp.exp(jnp.sum(lq1_ref[...] * lk1_ref[...], axis=1, keepdims=True))
           - jnp.exp(jnp.sum(lq2_ref[...] * lk2_ref[...], axis=1, keepdims=True))
           + lam_init)
    reps = 2 * d // V7X_LANES
    o1 = acc_ref[0] / _lane_tile(l_ref[0], reps)
    o2 = acc_ref[1] / _lane_tile(l_ref[1], reps)
    o = o1 - lam * o2
    o = _rmsnorm_f32(o, sub_ref[...]) * (1.0 - lam_init)
    o_ref[...] = o.astype(o_ref.dtype)


def _diff_attention(proj, lq1, lk1, lq2, lk2, subln, weights, layer, *, n_heads,
                    lam_init):
    b, s, _ = proj.shape
    hd = 2 * DIFF_HEAD_DIM
    tq = ATT_TQ
    assert s % tq == 0 and tq % (2 * ATT_CK) == 0
    nq = s // tq
    n_steps = b * n_heads * nq
    bf16_rows = 2 * V7X_SUBLANES

    def slab_specs(w):
        _, r, c = w.shape
        group = next(g for g in range(1, n_steps + 1)
                     if n_steps % g == 0 and r % (n_steps // g) == 0
                     and (r // (n_steps // g)) % bf16_rows == 0)
        rs = r // (n_steps // group)
        step = lambda bi, h, i: ((bi * n_heads + h) * nq + i) // group
        return (pl.BlockSpec((None, rs, c), lambda bi, h, i: (layer, step(bi, h, i), 0)),
                pl.BlockSpec((rs, c), lambda bi, h, i: (step(bi, h, i), 0)))

    specs = [slab_specs(w) for w in weights]
    in_slabs = [sp[0] for sp in specs]
    out_slabs = [sp[1] for sp in specs]
    vec = pl.BlockSpec((1, DIFF_HEAD_DIM), lambda bi, h, i: (0, 0))
    outs = pl.pallas_call(
        functools.partial(_diff_attn_kernel, lam_init=lam_init, n_cast=len(weights)),
        grid=(b, n_heads, nq),
        in_specs=[
            pl.BlockSpec((None, tq, hd), lambda bi, h, i: (bi, i, h)),
            pl.BlockSpec((None, s, hd), lambda bi, h, i: (bi, 0, n_heads + h)),
            pl.BlockSpec((None, s, hd), lambda bi, h, i: (bi, 0, 2 * n_heads + h)),
            vec, vec, vec, vec,
            pl.BlockSpec((1, hd), lambda bi, h, i: (0, 0)),
        ] + in_slabs,
        out_specs=[pl.BlockSpec((None, tq, hd), lambda bi, h, i: (bi, i, h))]
        + out_slabs,
        out_shape=[jax.ShapeDtypeStruct((b, s, n_heads * hd), BF16)]
        + [jax.ShapeDtypeStruct(w.shape[1:], BF16) for w in weights],
        scratch_shapes=[
            pltpu.VMEM((2, tq, V7X_LANES), F32),
            pltpu.VMEM((2, tq, V7X_LANES), F32),
            pltpu.VMEM((2, tq, hd), F32),
            pltpu.VMEM((2, 2, tq, ATT_CK), BF16),
            pltpu.VMEM((2, 2, tq, V7X_LANES), F32),
        ],
        compiler_params=_params(3),
        name="diff_attention",
    )(proj, proj, proj, lq1, lk1, lq2, lk2, subln, *weights)
    return outs[0], outs[1:]


def _pool_out_kernel(u_ref, uh_ref, oa_ref, x_ref, pw_ref, ps_ref, wo_ref, o_ref,
                     e_ref, a1_ref, a2_ref, a3_ref):
    tm, c = u_ref.shape
    halo = POOL_HALO
    n = tm + halo
    gd = c // len(POOL_WINDOWS)
    i = pl.program_id(1)

    u = u_ref[...].astype(F32)
    hist = uh_ref[...].astype(F32)
    e_ref[:halo, :] = jnp.where(i == 0, 0.0, hist)
    e_ref[halo:, :] = u
    a1_ref[8:, :] = e_ref[8:, :] + e_ref[7:n - 1, :]
    a2_ref[16:, :] = a1_ref[16:, gd:] + a1_ref[14:n - 2, gd:]
    a3_ref[24:, :] = a2_ref[24:, gd:] + a2_ref[20:n - 4, gd:]
    a4 = a3_ref[halo:, gd:] + a3_ref[halo - 8:n - 8, gd:]
    wsum = (a1_ref[halo:, :gd], a2_ref[halo:, :gd], a3_ref[halo:, :gd], a4)

    head = max(POOL_WINDOWS)
    pos1 = lax.broadcasted_iota(jnp.int32, (head, gd), 0) + (i * tm + 1)
    mixed = []
    for g, wl in enumerate(POOL_WINDOWS):
        ug = u[:, g * gd:(g + 1) * gd]
        inv_head = 1.0 / jnp.minimum(pos1, wl).astype(F32)
        pooled = jnp.concatenate([wsum[g][:head] * inv_head - ug[:head],
                                  wsum[g][head:] * (1.0 / wl) - ug[head:]], axis=0)
        mg = jnp.dot(pooled.astype(BF16), pw_ref[g], preferred_element_type=F32)
        mixed.append((mg * ps_ref[:, g * gd:(g + 1) * gd]).astype(BF16))
    mixed = jnp.concatenate(mixed, axis=1)

    da = oa_ref.shape[1]
    acc = jnp.dot(oa_ref[...], wo_ref[:da, :], preferred_element_type=F32)
    acc = acc + jnp.dot(mixed, wo_ref[da:, :], preferred_element_type=F32)
    o_ref[...] = x_ref[...] + acc


def _pool_out_proj(proj, o_attn, x, pool_w, pool_scale, w_o):
    b, s, d_model = x.shape
    d_attn = o_attn.shape[2]
    c = d_model - d_attn
    tm, halo = POOL_TM, POOL_HALO
    assert s % tm == 0 and tm % halo == 0 and proj.shape[2] == 3 * d_attn + c
    assert (3 * d_attn) % c == 0
    u_blk = 3 * d_attn // c
    hb = tm // halo
    gd = c // len(POOL_WINDOWS)
    n = tm + halo
    return pl.pallas_call(
        _pool_out_kernel,
        grid=(b, s // tm),
        in_specs=[
            pl.BlockSpec((None, tm, c), lambda bi, i: (bi, i, u_blk)),
            pl.BlockSpec((None, halo, c),
                         lambda bi, i: (bi, jnp.maximum(i * hb - 1, 0), u_blk)),
            pl.BlockSpec((None, tm, d_attn), lambda bi, i: (bi, i, 0)),
            pl.BlockSpec((None, tm, d_model), lambda bi, i: (bi, i, 0)),
            _resident(pool_w.shape),
            _resident((1, c)),
            _resident(w_o.shape),
        ],
        out_specs=pl.BlockSpec((None, tm, d_model), lambda bi, i: (bi, i, 0)),
        out_shape=jax.ShapeDtypeStruct((b, s, d_model), F32),
        scratch_shapes=[
            pltpu.VMEM((n, c), F32),
            pltpu.VMEM((n, c), F32),
            pltpu.VMEM((n, c - gd), F32),
            pltpu.VMEM((n, c - 2 * gd), F32),
        ],
        compiler_params=_params(2),
        name="pool_out_proj",
    )(proj, proj, o_attn, x, pool_w, pool_scale, w_o)


def _cross_attn_kernel(x_ref, g_ref, wq_ref, kv_ref, wo_ref, o_ref, oh_ref):
    d_model = x_ref.shape[1]
    hd = d_model // N_XATTN_HEADS
    x = x_ref[...]
    hx = _rmsnorm_f32(x, g_ref[...]).astype(BF16)
    qf = jnp.dot(hx, wq_ref[...], preferred_element_type=F32)
    qf = (qf * (hd ** -0.5 * LOG2E)).astype(BF16)
    for h in range(N_XATTN_HEADS):
        q = qf[:, h * hd:(h + 1) * hd]
        kh = kv_ref[:, h * hd:(h + 1) * hd]
        vh = kv_ref[:, d_model + h * hd:d_model + (h + 1) * hd]
        s = lax.dot_general(q, kh, (((1,), (1,)), ((), ())),
                            preferred_element_type=F32)
        p = jnp.exp2(s - jnp.max(s, axis=1, keepdims=True))
        l = jnp.sum(p, axis=1, keepdims=True)
        oh = jnp.dot(p.astype(BF16), vh, preferred_element_type=F32) / l
        oh_ref[:, h * hd:(h + 1) * hd] = oh.astype(oh_ref.dtype)
    o_ref[...] = x + jnp.dot(oh_ref[...], wo_ref[...], preferred_element_type=F32)


def _cross_attention(x, g, wq, kv, wo):
    b, s, d_model = x.shape
    tm = XATT_TM
    assert s % tm == 0
    return pl.pallas_call(
        _cross_attn_kernel,
        grid=(b, s // tm),
        in_specs=[
            pl.BlockSpec((None, tm, d_model), lambda bi, i: (bi, i, 0)),
            _resident((1, d_model)),
            _resident(wq.shape),
            pl.BlockSpec((None,) + kv.shape[1:], lambda bi, i: (bi, 0, 0)),
            _resident(wo.shape),
        ],
        out_specs=pl.BlockSpec((None, tm, d_model), lambda bi, i: (bi, i, 0)),
        out_shape=jax.ShapeDtypeStruct((b, s, d_model), F32),
        scratch_shapes=[pltpu.VMEM((tm, d_model), BF16)],
        compiler_params=_params(2),
        name="cross_attention",
    )(x, g, wq, kv, wo)


def _ffn_kernel(x_ref, g_ref, wg_ref, wu_ref, wd_ref, gf_ref, o_ref, h_ref, *,
                final_norm):
    j = pl.program_id(1)
    last = pl.num_programs(1) - 1

    def glu_down(h):
        gate = jnp.dot(h, wg_ref[...], preferred_element_type=F32)
        up = jnp.dot(h, wu_ref[...], preferred_element_type=F32)
        act = (gate / (1.0 + jnp.exp(-gate)) * up).astype(BF16)
        return jnp.dot(act, wd_ref[...], preferred_element_type=F32)

    @pl.when(j == 0)
    def _():
        x = x_ref[...]
        h_ref[...] = _rmsnorm_f32(x, g_ref[...]).astype(h_ref.dtype)
        o_ref[...] = x + glu_down(h_ref[...])

    @pl.when(jnp.logical_and(j > 0, j < last))
    def _():
        o_ref[...] += glu_down(h_ref[...])

    @pl.when(j == last)
    def _():
        y = o_ref[...] + glu_down(h_ref[...])
        o_ref[...] = _rmsnorm_f32(y, gf_ref[...]) if final_norm else y


def _ffn(x, g, w_gate, w_up, w_down, g_final, *, final_norm):
    m, d_model = x.shape
    f = w_gate.shape[1]
    tm, tf = FFN_TM, FFN_TF
    assert m % tm == 0 and f % tf == 0 and f // tf >= 2
    return pl.pallas_call(
        functools.partial(_ffn_kernel, final_norm=final_norm),
        grid=(m // tm, f // tf),
        in_specs=[
            pl.BlockSpec((tm, d_model), lambda i, j: (i, 0)),
            pl.BlockSpec((1, d_model), lambda i, j: (0, 0)),
            pl.BlockSpec((d_model, tf), lambda i, j: (0, j)),
            pl.BlockSpec((d_model, tf), lambda i, j: (0, j)),
            pl.BlockSpec((tf, d_model), lambda i, j: (j, 0)),
            pl.BlockSpec((1, d_model), lambda i, j: (0, 0)),
        ],
        out_specs=pl.BlockSpec((tm, d_model), lambda i, j: (i, 0)),
        out_shape=jax.ShapeDtypeStruct((m, d_model), F32),
        scratch_shapes=[pltpu.VMEM((tm, d_model), BF16)],
        compiler_params=_params(2),
        name="swiglu_final_norm",
    )(x, g, w_gate, w_up, w_down, g_final)


def kernel(x, mem, norm_mix, w_in, lambda_q1, lambda_k1, lambda_q2, lambda_k2, subln, pool_w, pool_scale, w_o, norm_xattn, norm_mem, wq_x, wkv_x, wo_x, norm_ffn, w_gate, w_up, w_down, norm_final):
    b, s, d_model = x.shape
    depth = norm_mix.shape[0]
    d_attn = d_model // 2
    n_diff_heads = d_attn // (2 * DIFF_HEAD_DIM)
    d_in = w_in.shape[2]
    t = b * s

    col_scale = jnp.concatenate([
        jnp.full((1, d_attn), DIFF_HEAD_DIM ** -0.5 * LOG2E, F32),
        jnp.ones((1, d_in - d_attn), F32)], axis=1)
    ones_kv = jnp.ones((1, 2 * d_model), F32)
    row = lambda v: v.reshape(1, -1).astype(F32)

    for l in range(depth):
        lam_init = _lambda_init(l)
        proj = _norm_matmul(x.reshape(t, d_model), row(norm_mix[l]),
                            w_in[l].astype(BF16), col_scale,
                            tm=MM_TM, tn=MM_TN, name="in_proj")
        proj = proj.reshape(b, s, d_in)
        later = (w_o, wkv_x, wq_x, wo_x, w_gate, w_up, w_down)
        o_attn, (w_o16, wkv16, wq16, wo16, wg16, wu16, wd16) = _diff_attention(
            proj, row(lambda_q1[l]), row(lambda_k1[l]), row(lambda_q2[l]),
            row(lambda_k2[l]), row(subln[l]), later, l,
            n_heads=n_diff_heads, lam_init=lam_init)
        pool_w16 = pool_w[l].astype(BF16)
        x = _pool_out_proj(proj, o_attn, x, pool_w16, row(pool_scale[l]), w_o16)
        kv = _norm_matmul(mem.reshape(b * MEM_LEN, d_model), row(norm_mem[l]),
                          wkv16, ones_kv, tm=b * MEM_LEN, tn=MM_TN, name="mem_kv_proj")
        x = _cross_attention(x, row(norm_xattn[l]), wq16,
                             kv.reshape(b, MEM_LEN, 2 * d_model), wo16)
        x = _ffn(x.reshape(t, d_model), row(norm_ffn[l]), wg16, wu16, wd16,
                 row(norm_final), final_norm=(l == depth - 1)).reshape(b, s, d_model)
    return x
```

```python
import functools
import math

import jax
import jax.numpy as jnp
from jax import lax
from jax.experimental import pallas as pl
from jax.experimental.pallas import tpu as pltpu

MEM_LEN = 256
DIFF_HEAD_DIM = 128
POOL_WINDOWS = (2, 4, 8, 16)
N_XATTN_HEADS = 4
EPS = 1e-6
NEG_INF = -1e30
LOG2E = math.log2(math.e)

V7X_VMEM_BYTES = 64 * 1024 * 1024
V7X_LANES = 128
V7X_SUBLANES = 8
VMEM_LIMIT_BYTES = V7X_VMEM_BYTES - 8 * 1024 * 1024

MM_TM = 512
MM_TN = 1024
ATT_TQ = 1024
ATT_CK = 512
POOL_TM = 512
POOL_HALO = 32
XATT_TM = 512
FFN_TM = 1024
FFN_TF = 512

BF16 = jnp.bfloat16
F32 = jnp.float32


def _lambda_init(layer_idx):
    return 0.8 - 0.6 * math.exp(-0.3 * layer_idx)


def _rmsnorm_f32(x, g):
    y = x * lax.rsqrt(jnp.mean(x * x, axis=-1, keepdims=True) + EPS)
    return y * g


def _lane_tile(x, reps):
    return x if reps == 1 else jnp.concatenate([x] * reps, axis=1)


def _params(n_axes):
    return pltpu.CompilerParams(
        dimension_semantics=("arbitrary",) * n_axes,
        vmem_limit_bytes=VMEM_LIMIT_BYTES,
    )


def _resident(shape):
    nd = len(shape)
    return pl.BlockSpec(shape, lambda *_: (0,) * nd, pipeline_mode=pl.Buffered(1))


def _norm_matmul_kernel(x_ref, g_ref, w_ref, cs_ref, o_ref, *, tn):
    h = _rmsnorm_f32(x_ref[...], g_ref[...]).astype(BF16)
    for c0 in range(0, o_ref.shape[1], tn):
        acc = jnp.dot(h, w_ref[:, c0:c0 + tn], preferred_element_type=F32)
        o_ref[:, c0:c0 + tn] = (acc * cs_ref[:, c0:c0 + tn]).astype(o_ref.dtype)


def _norm_matmul(x, g, w, col_scale, *, tm, tn, name):
    m, k = x.shape
    n = w.shape[1]
    assert m % tm == 0 and n % tn == 0
    return pl.pallas_call(
        functools.partial(_norm_matmul_kernel, tn=tn),
        grid=(m // tm,),
        in_specs=[
            pl.BlockSpec((tm, k), lambda i: (i, 0)),
            _resident((1, k)),
            _resident((k, n)),
            _resident((1, n)),
        ],
        out_specs=pl.BlockSpec((tm, n), lambda i: (i, 0)),
        out_shape=jax.ShapeDtypeStruct((m, n), BF16),
        compiler_params=_params(1),
        name=name,
    )(x, g, w, col_scale)


def _diff_attn_kernel(q_ref, k_ref, v_ref, lq1_ref, lk1_ref, lq2_ref, lk2_ref,
                      sub_ref, *refs, lam_init, n_cast):
    w32_refs, o_ref, w16_refs = refs[:n_cast], refs[n_cast], refs[n_cast + 1:2 * n_cast + 1]
    m_ref, l_ref, acc_ref, p_ref, a_ref = refs[2 * n_cast + 1:]
    for w32, w16 in zip(w32_refs, w16_refs):
        w16[...] = w32[...].astype(w16.dtype)

    d = DIFF_HEAD_DIM
    tq = q_ref.shape[0]
    ck = ATT_CK
    n_band = tq // ck
    qi = pl.program_id(2)
    n_full = qi * n_band

    m_ref[...] = jnp.full(m_ref.shape, NEG_INF, F32)
    l_ref[...] = jnp.zeros(l_ref.shape, F32)
    acc_ref[...] = jnp.zeros(acc_ref.shape, F32)

    def probs(j, slot, row0, row1, masked):
        rows = row1 - row0
        kstart = pl.multiple_of(j * ck, ck)
        if masked:
            r_id = lax.broadcasted_iota(jnp.int32, (rows, ck), 0)
            c_id = lax.broadcasted_iota(jnp.int32, (rows, ck), 1)
            visible = c_id <= r_id
        for half in range(2):
            q = q_ref[row0:row1, half * d:(half + 1) * d]
            k = k_ref[pl.ds(kstart, ck), half * d:(half + 1) * d]
            s = lax.dot_general(q, k, (((1,), (1,)), ((), ())),
                                preferred_element_type=F32)
            if masked:
                s = jnp.where(visible, s, NEG_INF)
            m_prev = m_ref[half, row0:row1, :]
            m_new = jnp.maximum(m_prev, jnp.max(s, axis=1, keepdims=True))
            alpha = jnp.exp2(m_prev - m_new)
            p = jnp.exp2(s - _lane_tile(m_new, ck // V7X_LANES))
            l_ref[half, row0:row1, :] = (alpha * l_ref[half, row0:row1, :]
                                         + jnp.sum(p, axis=1, keepdims=True))
            m_ref[half, row0:row1, :] = m_new
            a_ref[slot, half, row0:row1, :] = alpha
            p_ref[slot, half, row0:row1, :] = p.astype(BF16)

    def values(j, slot, row0, row1):
        kstart = pl.multiple_of(j * ck, ck)
        v = v_ref[pl.ds(kstart, ck), :]
        for half in range(2):
            pv = jnp.dot(p_ref[slot, half, row0:row1, :], v, preferred_element_type=F32)
            alpha = _lane_tile(a_ref[slot, half, row0:row1, :], 2 * d // V7X_LANES)
            acc_ref[half, row0:row1, :] = acc_ref[half, row0:row1, :] * alpha + pv

    for i in range(n_band):
        band = n_band - 1 - i
        for r in range(band, n_band):
            probs(n_full + band, i % 2, r * ck, (r + 1) * ck, r == band)
            if i > 0 and r > band:
                values(n_full + band + 1, (i - 1) % 2, r * ck, (r + 1) * ck)

    def pair_step(j_new, slot_new, j_old):
        for r0 in range(0, tq, ck):
            probs(j_new, slot_new, r0, r0 + ck, False)
            values(j_old, 1 - slot_new, r0, r0 + ck)

    def body(t, carry):
        j = 2 * t
        pair_step(j, 0, jnp.where(t == 0, n_full, j - 1))
        pair_step(j + 1, 1, j)
        return carry

    lax.fori_loop(0, qi * (n_band // 2), body, 0)

    lam = (jnp.exp(jnp.sum(lq1_ref[...] * lk1_ref[...], axis=1, keepdims=True))
           - jnp.exp(jnp.sum(lq2_ref[...] * lk2_ref[...], axis=1, keepdims=True))
           + lam_init)
    reps = 2 * d // V7X_LANES
    j_last = jnp.where(n_full == 0, n_full, n_full - 1)
    for r0 in range(0, tq, ck):
        r1 = r0 + ck
        values(j_last, 1, r0, r1)
        o1 = acc_ref[0, r0:r1, :] / _lane_tile(l_ref[0, r0:r1, :], reps)
        o2 = acc_ref[1, r0:r1, :] / _lane_tile(l_ref[1, r0:r1, :], reps)
        o = _rmsnorm_f32(o1 - lam * o2, sub_ref[...]) * (1.0 - lam_init)
        o_ref[r0:r1, :] = o.astype(o_ref.dtype)


def _diff_attention(proj, lq1, lk1, lq2, lk2, subln, weights, layer, *, n_heads,
                    lam_init):
    b, s, _ = proj.shape
    hd = 2 * DIFF_HEAD_DIM
    tq = ATT_TQ
    assert s % tq == 0 and tq % (2 * ATT_CK) == 0
    nq = s // tq
    n_steps = b * n_heads * nq
    bf16_rows = 2 * V7X_SUBLANES

    def slab_specs(w):
        _, r, c = w.shape
        group = next(g for g in range(1, n_steps + 1)
                     if n_steps % g == 0 and r % (n_steps // g) == 0
                     and (r // (n_steps // g)) % bf16_rows == 0)
        rs = r // (n_steps // group)
        step = lambda bi, h, i: ((bi * n_heads + h) * nq + i) // group
        return (pl.BlockSpec((None, rs, c), lambda bi, h, i: (layer, step(bi, h, i), 0)),
                pl.BlockSpec((rs, c), lambda bi, h, i: (step(bi, h, i), 0)))

    specs = [slab_specs(w) for w in weights]
    in_slabs = [sp[0] for sp in specs]
    out_slabs = [sp[1] for sp in specs]
    vec = pl.BlockSpec((1, DIFF_HEAD_DIM), lambda bi, h, i: (0, 0))
    outs = pl.pallas_call(
        functools.partial(_diff_attn_kernel, lam_init=lam_init, n_cast=len(weights)),
        grid=(b, n_heads, nq),
        in_specs=[
            pl.BlockSpec((None, tq, hd), lambda bi, h, i: (bi, i, h)),
            pl.BlockSpec((None, s, hd), lambda bi, h, i: (bi, 0, n_heads + h)),
            pl.BlockSpec((None, s, hd), lambda bi, h, i: (bi, 0, 2 * n_heads + h)),
            vec, vec, vec, vec,
            pl.BlockSpec((1, hd), lambda bi, h, i: (0, 0)),
        ] + in_slabs,
        out_specs=[pl.BlockSpec((None, tq, hd), lambda bi, h, i: (bi, i, h))]
        + out_slabs,
        out_shape=[jax.ShapeDtypeStruct((b, s, n_heads * hd), BF16)]
        + [jax.ShapeDtypeStruct(w.shape[1:], BF16) for w in weights],
        scratch_shapes=[
            pltpu.VMEM((2, tq, V7X_LANES), F32),
            pltpu.VMEM((2, tq, V7X_LANES), F32),
            pltpu.VMEM((2, tq, hd), F32),
            pltpu.VMEM((2, 2, tq, ATT_CK), BF16),
            pltpu.VMEM((2, 2, tq, V7X_LANES), F32),
        ],
        compiler_params=_params(3),
        name="diff_attention",
    )(proj, proj, proj, lq1, lk1, lq2, lk2, subln, *weights)
    return outs[0], outs[1:]


def _pool_out_kernel(u_ref, uh_ref, oa_ref, x_ref, pw_ref, ps_ref, wo_ref, o_ref,
                     e_ref, a1_ref, a2_ref, a3_ref):
    tm, c = u_ref.shape
    halo = POOL_HALO
    n = tm + halo
    gd = c // len(POOL_WINDOWS)
    i = pl.program_id(1)

    u = u_ref[...].astype(F32)
    hist = uh_ref[...].astype(F32)
    e_ref[:halo, :] = jnp.where(i == 0, 0.0, hist)
    e_ref[halo:, :] = u
    a1_ref[8:, :] = e_ref[8:, :] + e_ref[7:n - 1, :]
    a2_ref[16:, :] = a1_ref[16:, gd:] + a1_ref[14:n - 2, gd:]
    a3_ref[24:, :] = a2_ref[24:, gd:] + a2_ref[20:n - 4, gd:]
    a4 = a3_ref[halo:, gd:] + a3_ref[halo - 8:n - 8, gd:]
    wsum = (a1_ref[halo:, :gd], a2_ref[halo:, :gd], a3_ref[halo:, :gd], a4)

    head = max(POOL_WINDOWS)
    pos1 = lax.broadcasted_iota(jnp.int32, (head, gd), 0) + (i * tm + 1)
    mixed = []
    for g, wl in enumerate(POOL_WINDOWS):
        ug = u[:, g * gd:(g + 1) * gd]
        inv_head = 1.0 / jnp.minimum(pos1, wl).astype(F32)
        pooled = jnp.concatenate([wsum[g][:head] * inv_head - ug[:head],
                                  wsum[g][head:] * (1.0 / wl) - ug[head:]], axis=0)
        mg = jnp.dot(pooled.astype(BF16), pw_ref[g], preferred_element_type=F32)
        mixed.append((mg * ps_ref[:, g * gd:(g + 1) * gd]).astype(BF16))
    mixed = jnp.concatenate(mixed, axis=1)

    da = oa_ref.shape[1]
    acc = jnp.dot(oa_ref[...], wo_ref[:da, :], preferred_element_type=F32)
    acc = acc + jnp.dot(mixed, wo_ref[da:, :], preferred_element_type=F32)
    o_ref[...] = x_ref[...] + acc


def _pool_out_proj(proj, o_attn, x, pool_w, pool_scale, w_o):
    b, s, d_model = x.shape
    d_attn = o_attn.shape[2]
    c = d_model - d_attn
    tm, halo = POOL_TM, POOL_HALO
    assert s % tm == 0 and tm % halo == 0 and proj.shape[2] == 3 * d_attn + c
    assert (3 * d_attn) % c == 0
    u_blk = 3 * d_attn // c
    hb = tm // halo
    gd = c // len(POOL_WINDOWS)
    n = tm + halo
    return pl.pallas_call(
        _pool_out_kernel,
        grid=(b, s // tm),
        in_specs=[
            pl.BlockSpec((None, tm, c), lambda bi, i: (bi, i, u_blk)),
            pl.BlockSpec((None, halo, c),
                         lambda bi, i: (bi, jnp.maximum(i * hb - 1, 0), u_blk)),
            pl.BlockSpec((None, tm, d_attn), lambda bi, i: (bi, i, 0)),
            pl.BlockSpec((None, tm, d_model), lambda bi, i: (bi, i, 0)),
            _resident(pool_w.shape),
            _resident((1, c)),
            _resident(w_o.shape),
        ],
        out_specs=pl.BlockSpec((None, tm, d_model), lambda bi, i: (bi, i, 0)),
        out_shape=jax.ShapeDtypeStruct((b, s, d_model), F32),
        scratch_shapes=[
            pltpu.VMEM((n, c), F32),
            pltpu.VMEM((n, c), F32),
            pltpu.VMEM((n, c - gd), F32),
            pltpu.VMEM((n, c - 2 * gd), F32),
        ],
        compiler_params=_params(2),
        name="pool_out_proj",
    )(proj, proj, o_attn, x, pool_w, pool_scale, w_o)


def _cross_attn_kernel(x_ref, g_ref, wq_ref, kv_ref, wo_ref, o_ref, oh_ref):
    d_model = x_ref.shape[1]
    hd = d_model // N_XATTN_HEADS
    x = x_ref[...]
    hx = _rmsnorm_f32(x, g_ref[...]).astype(BF16)
    qf = jnp.dot(hx, wq_ref[...], preferred_element_type=F32)
    qf = (qf * (hd ** -0.5 * LOG2E)).astype(BF16)
    scores = []
    for h in range(N_XATTN_HEADS):
        kh = kv_ref[:, h * hd:(h + 1) * hd]
        scores.append(lax.dot_general(qf[:, h * hd:(h + 1) * hd], kh,
                                      (((1,), (1,)), ((), ())),
                                      preferred_element_type=F32))
    for h in range(N_XATTN_HEADS):
        s = scores[h]
        vh = kv_ref[:, d_model + h * hd:d_model + (h + 1) * hd]
        p = jnp.exp2(s - jnp.max(s, axis=1, keepdims=True))
        l = jnp.sum(p, axis=1, keepdims=True)
        oh = jnp.dot(p.astype(BF16), vh, preferred_element_type=F32) / l
        oh_ref[:, h * hd:(h + 1) * hd] = oh.astype(oh_ref.dtype)
    o_ref[...] = x + jnp.dot(oh_ref[...], wo_ref[...], preferred_element_type=F32)


def _cross_attention(x, g, wq, kv, wo):
    b, s, d_model = x.shape
    tm = XATT_TM
    assert s % tm == 0
    return pl.pallas_call(
        _cross_attn_kernel,
        grid=(b, s // tm),
        in_specs=[
            pl.BlockSpec((None, tm, d_model), lambda bi, i: (bi, i, 0)),
            _resident((1, d_model)),
            _resident(wq.shape),
            pl.BlockSpec((None,) + kv.shape[1:], lambda bi, i: (bi, 0, 0)),
            _resident(wo.shape),
        ],
        out_specs=pl.BlockSpec((None, tm, d_model), lambda bi, i: (bi, i, 0)),
        out_shape=jax.ShapeDtypeStruct((b, s, d_model), F32),
        scratch_shapes=[pltpu.VMEM((tm, d_model), BF16)],
        compiler_params=_params(2),
        name="cross_attention",
    )(x, g, wq, kv, wo)


def _ffn_kernel(x_ref, g_ref, wg_ref, wu_ref, wd_ref, gf_ref, o_ref, h_ref, *,
                final_norm):
    j = pl.program_id(1)
    last = pl.num_programs(1) - 1

    def glu_down(h):
        gate = jnp.dot(h, wg_ref[...], preferred_element_type=F32)
        up = jnp.dot(h, wu_ref[...], preferred_element_type=F32)
        act = (gate / (1.0 + jnp.exp(-gate)) * up).astype(BF16)
        return jnp.dot(act, wd_ref[...], preferred_element_type=F32)

    @pl.when(j == 0)
    def _():
        x = x_ref[...]
        h_ref[...] = _rmsnorm_f32(x, g_ref[...]).astype(h_ref.dtype)
        o_ref[...] = x + glu_down(h_ref[...])

    @pl.when(jnp.logical_and(j > 0, j < last))
    def _():
        o_ref[...] += glu_down(h_ref[...])

    @pl.when(j == last)
    def _():
        y = o_ref[...] + glu_down(h_ref[...])
        o_ref[...] = _rmsnorm_f32(y, gf_ref[...]) if final_norm else y


def _ffn(x, g, w_gate, w_up, w_down, g_final, *, final_norm):
    m, d_model = x.shape
    f = w_gate.shape[1]
    tm, tf = FFN_TM, FFN_TF
    assert m % tm == 0 and f % tf == 0 and f // tf >= 2
    return pl.pallas_call(
        functools.partial(_ffn_kernel, final_norm=final_norm),
        grid=(m // tm, f // tf),
        in_specs=[
            pl.BlockSpec((tm, d_model), lambda i, j: (i, 0)),
            pl.BlockSpec((1, d_model), lambda i, j: (0, 0)),
            pl.BlockSpec((d_model, tf), lambda i, j: (0, j)),
            pl.BlockSpec((d_model, tf), lambda i, j: (0, j)),
            pl.BlockSpec((tf, d_model), lambda i, j: (j, 0)),
            pl.BlockSpec((1, d_model), lambda i, j: (0, 0)),
        ],
        out_specs=pl.BlockSpec((tm, d_model), lambda i, j: (i, 0)),
        out_shape=jax.ShapeDtypeStruct((m, d_model), F32),
        scratch_shapes=[pltpu.VMEM((tm, d_model), BF16)],
        compiler_params=_params(2),
        name="swiglu_final_norm",
    )(x, g, w_gate, w_up, w_down, g_final)


def kernel(x, mem, norm_mix, w_in, lambda_q1, lambda_k1, lambda_q2, lambda_k2, subln, pool_w, pool_scale, w_o, norm_xattn, norm_mem, wq_x, wkv_x, wo_x, norm_ffn, w_gate, w_up, w_down, norm_final):
    b, s, d_model = x.shape
    depth = norm_mix.shape[0]
    d_attn = d_model // 2
    n_diff_heads = d_attn // (2 * DIFF_HEAD_DIM)
    d_in = w_in.shape[2]
    t = b * s

    col_scale = jnp.concatenate([
        jnp.full((1, d_attn), DIFF_HEAD_DIM ** -0.5 * LOG2E, F32),
        jnp.ones((1, d_in - d_attn), F32)], axis=1)
    ones_kv = jnp.ones((1, 2 * d_model), F32)
    row = lambda v: v.reshape(1, -1).astype(F32)

    for l in range(depth):
        lam_init = _lambda_init(l)
        proj = _norm_matmul(x.reshape(t, d_model), row(norm_mix[l]),
                            w_in[l].astype(BF16), col_scale,
                            tm=MM_TM, tn=MM_TN, name="in_proj")
        proj = proj.reshape(b, s, d_in)
        later = (w_o, wkv_x, wq_x, wo_x, w_gate, w_up, w_down)
        o_attn, (w_o16, wkv16, wq16, wo16, wg16, wu16, wd16) = _diff_attention(
            proj, row(lambda_q1[l]), row(lambda_k1[l]), row(lambda_q2[l]),
            row(lambda_k2[l]), row(subln[l]), later, l,
            n_heads=n_diff_heads, lam_init=lam_init)
        pool_w16 = pool_w[l].astype(BF16)
        x = _pool_out_proj(proj, o_attn, x, pool_w16, row(pool_scale[l]), w_o16)
        kv = _norm_matmul(mem.reshape(b * MEM_LEN, d_model), row(norm_mem[l]),
                          wkv16, ones_kv, tm=b * MEM_LEN, tn=MM_TN, name="mem_kv_proj")
        x = _cross_attention(x, row(norm_xattn[l]), wq16,
                             kv.reshape(b, MEM_LEN, 2 * d_model), wo16)
        x = _ffn(x.reshape(t, d_model), row(norm_ffn[l]), wg16, wu16, wd16,
                 row(norm_final), final_norm=(l == depth - 1)).reshape(b, s, d_model)
    return x
```

```python
import functools
import math

import jax
import jax.numpy as jnp
from jax import lax
from jax.experimental import pallas as pl
from jax.experimental.pallas import tpu as pltpu

MEM_LEN = 256
DIFF_HEAD_DIM = 128
POOL_WINDOWS = (2, 4, 8, 16)
N_XATTN_HEADS = 4
EPS = 1e-6
NEG_INF = -1e30
LOG2E = math.log2(math.e)

V7X_VMEM_BYTES = 64 * 1024 * 1024
V7X_LANES = 128
V7X_SUBLANES = 8
VMEM_LIMIT_BYTES = V7X_VMEM_BYTES - 8 * 1024 * 1024

MM_TM = 1024
MM_TN = 1024
ATT_TQ = 1024
ATT_CK = 512
POOL_TM = 512
POOL_HALO = 32
XATT_TM = 512
FFN_TM = 1024
FFN_TF = 512

BF16 = jnp.bfloat16
F32 = jnp.float32


def _lambda_init(layer_idx):
    return 0.8 - 0.6 * math.exp(-0.3 * layer_idx)


def _rmsnorm_f32(x, g):
    y = x * lax.rsqrt(jnp.mean(x * x, axis=-1, keepdims=True) + EPS)
    return y * g


def _lane_tile(x, reps):
    return x if reps == 1 else jnp.concatenate([x] * reps, axis=1)


def _params(n_axes):
    return pltpu.CompilerParams(
        dimension_semantics=("arbitrary",) * n_axes,
        vmem_limit_bytes=VMEM_LIMIT_BYTES,
    )


def _resident(shape):
    nd = len(shape)
    return pl.BlockSpec(shape, lambda *_: (0,) * nd, pipeline_mode=pl.Buffered(1))


def _norm_matmul_kernel(x_ref, g_ref, w_ref, o_ref, *, tn, scale, n_scaled):
    h = _rmsnorm_f32(x_ref[...], g_ref[...]).astype(BF16)
    for c0 in range(0, o_ref.shape[1], tn):
        acc = jnp.dot(h, w_ref[:, c0:c0 + tn], preferred_element_type=F32)
        if c0 < n_scaled:
            acc = acc * scale
        o_ref[:, c0:c0 + tn] = acc.astype(o_ref.dtype)


def _norm_matmul(x, g, w, *, tm, tn, name, scale=1.0, n_scaled=0):
    m, k = x.shape
    n = w.shape[1]
    assert m % tm == 0 and n % tn == 0 and n_scaled % tn == 0
    return pl.pallas_call(
        functools.partial(_norm_matmul_kernel, tn=tn, scale=scale, n_scaled=n_scaled),
        grid=(m // tm,),
        in_specs=[
            pl.BlockSpec((tm, k), lambda i: (i, 0)),
            _resident((1, k)),
            _resident((k, n)),
        ],
        out_specs=pl.BlockSpec((tm, n), lambda i: (i, 0)),
        out_shape=jax.ShapeDtypeStruct((m, n), BF16),
        compiler_params=_params(1),
        name=name,
    )(x, g, w)


def _diff_attn_kernel(q_ref, k_ref, v_ref, lq1_ref, lk1_ref, lq2_ref, lk2_ref,
                      sub_ref, *refs, lam_init, n_cast):
    w32_refs, o_ref, w16_refs = refs[:n_cast], refs[n_cast], refs[n_cast + 1:2 * n_cast + 1]
    m_ref, l_ref, acc_ref, p_ref, a_ref = refs[2 * n_cast + 1:]
    d = DIFF_HEAD_DIM
    tq = q_ref.shape[0]
    ck = ATT_CK
    n_band = tq // ck
    qi = pl.program_id(2)
    n_full = qi * n_band

    m_ref[...] = jnp.full(m_ref.shape, NEG_INF, F32)
    l_ref[...] = jnp.zeros(l_ref.shape, F32)
    acc_ref[...] = jnp.zeros(acc_ref.shape, F32)

    def probs(j, slot, row0, row1, masked):
        rows = row1 - row0
        kstart = pl.multiple_of(j * ck, ck)
        if masked:
            r_id = lax.broadcasted_iota(jnp.int32, (rows, ck), 0)
            c_id = lax.broadcasted_iota(jnp.int32, (rows, ck), 1)
            visible = c_id <= r_id
        for half in range(2):
            q = q_ref[row0:row1, half * d:(half + 1) * d]
            k = k_ref[pl.ds(kstart, ck), half * d:(half + 1) * d]
            s = lax.dot_general(q, k, (((1,), (1,)), ((), ())),
                                preferred_element_type=F32)
            if masked:
                s = jnp.where(visible, s, NEG_INF)
            m_prev = m_ref[half, row0:row1, :]
            m_new = jnp.maximum(m_prev, jnp.max(s, axis=1, keepdims=True))
            alpha = jnp.exp2(m_prev - m_new)
            p = jnp.exp2(s - _lane_tile(m_new, ck // V7X_LANES))
            l_ref[half, row0:row1, :] = (alpha * l_ref[half, row0:row1, :]
                                         + jnp.sum(p, axis=1, keepdims=True))
            m_ref[half, row0:row1, :] = m_new
            a_ref[slot, half, row0:row1, :] = alpha
            p_ref[slot, half, row0:row1, :] = p.astype(BF16)

    def values(j, slot, row0, row1):
        kstart = pl.multiple_of(j * ck, ck)
        v = v_ref[pl.ds(kstart, ck), :]
        for half in range(2):
            pv = jnp.dot(p_ref[slot, half, row0:row1, :], v, preferred_element_type=F32)
            alpha = _lane_tile(a_ref[slot, half, row0:row1, :], 2 * d // V7X_LANES)
            acc_ref[half, row0:row1, :] = acc_ref[half, row0:row1, :] * alpha + pv

    for i in range(n_band):
        band = n_band - 1 - i
        for r in range(band, n_band):
            probs(n_full + band, i % 2, r * ck, (r + 1) * ck, r == band)
            if i > 0 and r > band:
                values(n_full + band + 1, (i - 1) % 2, r * ck, (r + 1) * ck)

    def pair_step(j_new, slot_new, j_old):
        for r0 in range(0, tq, ck):
            probs(j_new, slot_new, r0, r0 + ck, False)
            values(j_old, 1 - slot_new, r0, r0 + ck)

    def body(t, carry):
        j = 2 * t
        pair_step(j, 0, jnp.where(t == 0, n_full, j - 1))
        pair_step(j + 1, 1, j)
        return carry

    lax.fori_loop(0, qi * (n_band // 2), body, 0)

    lam = (jnp.exp(jnp.sum(lq1_ref[...] * lk1_ref[...], axis=1, keepdims=True))
           - jnp.exp(jnp.sum(lq2_ref[...] * lk2_ref[...], axis=1, keepdims=True))
           + lam_init)
    reps = 2 * d // V7X_LANES
    j_last = jnp.where(n_full == 0, n_full, n_full - 1)
    for r0 in range(0, tq, ck):
        r1 = r0 + ck
        values(j_last, 1, r0, r1)
        o1 = acc_ref[0, r0:r1, :] / _lane_tile(l_ref[0, r0:r1, :], reps)
        o2 = acc_ref[1, r0:r1, :] / _lane_tile(l_ref[1, r0:r1, :], reps)
        o = _rmsnorm_f32(o1 - lam * o2, sub_ref[...]) * (1.0 - lam_init)
        o_ref[r0:r1, :] = o.astype(o_ref.dtype)

    for w32, w16 in zip(w32_refs, w16_refs):
        w16[...] = w32[...].astype(w16.dtype)


def _diff_attention(proj, lq1, lk1, lq2, lk2, subln, weights, layer, *, n_heads,
                    lam_init):
    b, s, _ = proj.shape
    hd = 2 * DIFF_HEAD_DIM
    tq = ATT_TQ
    assert s % tq == 0 and tq % (2 * ATT_CK) == 0
    nq = s // tq
    n_steps = b * n_heads * nq
    bf16_rows = 2 * V7X_SUBLANES

    def slab_specs(w):
        _, r, c = w.shape
        group = next(g for g in range(1, n_steps + 1)
                     if n_steps % g == 0 and r % (n_steps // g) == 0
                     and (r // (n_steps // g)) % bf16_rows == 0)
        rs = r // (n_steps // group)
        step = lambda bi, h, i: ((bi * n_heads + h) * nq + i) // group
        return (pl.BlockSpec((None, rs, c), lambda bi, h, i: (layer, step(bi, h, i), 0)),
                pl.BlockSpec((rs, c), lambda bi, h, i: (step(bi, h, i), 0)))

    specs = [slab_specs(w) for w in weights]
    in_slabs = [sp[0] for sp in specs]
    out_slabs = [sp[1] for sp in specs]
    vec = pl.BlockSpec((1, DIFF_HEAD_DIM), lambda bi, h, i: (0, 0))
    outs = pl.pallas_call(
        functools.partial(_diff_attn_kernel, lam_init=lam_init, n_cast=len(weights)),
        grid=(b, n_heads, nq),
        in_specs=[
            pl.BlockSpec((None, tq, hd), lambda bi, h, i: (bi, i, h)),
            pl.BlockSpec((None, s, hd), lambda bi, h, i: (bi, 0, n_heads + h)),
            pl.BlockSpec((None, s, hd), lambda bi, h, i: (bi, 0, 2 * n_heads + h)),
            vec, vec, vec, vec,
            pl.BlockSpec((1, hd), lambda bi, h, i: (0, 0)),
        ] + in_slabs,
        out_specs=[pl.BlockSpec((None, tq, hd), lambda bi, h, i: (bi, i, h))]
        + out_slabs,
        out_shape=[jax.ShapeDtypeStruct((b, s, n_heads * hd), BF16)]
        + [jax.ShapeDtypeStruct(w.shape[1:], BF16) for w in weights],
        scratch_shapes=[
            pltpu.VMEM((2, tq, V7X_LANES), F32),
            pltpu.VMEM((2, tq, V7X_LANES), F32),
            pltpu.VMEM((2, tq, hd), F32),
            pltpu.VMEM((2, 2, tq, ATT_CK), BF16),
            pltpu.VMEM((2, 2, tq, V7X_LANES), F32),
        ],
        compiler_params=_params(3),
        name="diff_attention",
    )(proj, proj, proj, lq1, lk1, lq2, lk2, subln, *weights)
    return outs[0], outs[1:]


def _pool_out_kernel(u_ref, uh_ref, oa_ref, x_ref, pw_ref, ps_ref, wo_ref, o_ref,
                     e_ref, a1_ref, a2_ref, a3_ref):
    tm, c = u_ref.shape
    halo = POOL_HALO
    n = tm + halo
    gd = c // len(POOL_WINDOWS)
    i = pl.program_id(1)

    u = u_ref[...].astype(F32)
    hist = uh_ref[...].astype(F32)
    e_ref[:halo, :] = jnp.where(i == 0, 0.0, hist)
    e_ref[halo:, :] = u
    a1_ref[8:, :] = e_ref[8:, :] + e_ref[7:n - 1, :]
    a2_ref[16:, :] = a1_ref[16:, gd:] + a1_ref[14:n - 2, gd:]
    a3_ref[24:, :] = a2_ref[24:, gd:] + a2_ref[20:n - 4, gd:]
    a4 = a3_ref[halo:, gd:] + a3_ref[halo - 8:n - 8, gd:]
    wsum = (a1_ref[halo:, :gd], a2_ref[halo:, :gd], a3_ref[halo:, :gd], a4)

    head = max(POOL_WINDOWS)
    pos1 = lax.broadcasted_iota(jnp.int32, (head, gd), 0) + (i * tm + 1)
    mixed = []
    for g, wl in enumerate(POOL_WINDOWS):
        ug = u[:, g * gd:(g + 1) * gd]
        inv_head = 1.0 / jnp.minimum(pos1, wl).astype(F32)
        pooled = jnp.concatenate([wsum[g][:head] * inv_head - ug[:head],
                                  wsum[g][head:] * (1.0 / wl) - ug[head:]], axis=0)
        mg = jnp.dot(pooled.astype(BF16), pw_ref[g], preferred_element_type=F32)
        mixed.append((mg * ps_ref[:, g * gd:(g + 1) * gd]).astype(BF16))
    mixed = jnp.concatenate(mixed, axis=1)

    da = oa_ref.shape[1]
    acc = jnp.dot(oa_ref[...], wo_ref[:da, :], preferred_element_type=F32)
    acc = acc + jnp.dot(mixed, wo_ref[da:, :], preferred_element_type=F32)
    o_ref[...] = x_ref[...] + acc


def _pool_out_proj(proj, o_attn, x, pool_w, pool_scale, w_o):
    b, s, d_model = x.shape
    d_attn = o_attn.shape[2]
    c = d_model - d_attn
    tm, halo = POOL_TM, POOL_HALO
    assert s % tm == 0 and tm % halo == 0 and proj.shape[2] == 3 * d_attn + c
    assert (3 * d_attn) % c == 0
    u_blk = 3 * d_attn // c
    hb = tm // halo
    gd = c // len(POOL_WINDOWS)
    n = tm + halo
    return pl.pallas_call(
        _pool_out_kernel,
        grid=(b, s // tm),
        in_specs=[
            pl.BlockSpec((None, tm, c), lambda bi, i: (bi, i, u_blk)),
            pl.BlockSpec((None, halo, c),
                         lambda bi, i: (bi, jnp.maximum(i * hb - 1, 0), u_blk)),
            pl.BlockSpec((None, tm, d_attn), lambda bi, i: (bi, i, 0)),
            pl.BlockSpec((None, tm, d_model), lambda bi, i: (bi, i, 0)),
            _resident(pool_w.shape),
            _resident((1, c)),
            _resident(w_o.shape),
        ],
        out_specs=pl.BlockSpec((None, tm, d_model), lambda bi, i: (bi, i, 0)),
        out_shape=jax.ShapeDtypeStruct((b, s, d_model), F32),
        scratch_shapes=[
            pltpu.VMEM((n, c), F32),
            pltpu.VMEM((n, c), F32),
            pltpu.VMEM((n, c - gd), F32),
            pltpu.VMEM((n, c - 2 * gd), F32),
        ],
        compiler_params=_params(2),
        name="pool_out_proj",
    )(proj, proj, o_attn, x, pool_w, pool_scale, w_o)


def _cross_attn_kernel(x_ref, g_ref, wq_ref, kv_ref, wo_ref, o_ref, oh_ref):
    d_model = x_ref.shape[1]
    hd = d_model // N_XATTN_HEADS
    x = x_ref[...]
    hx = _rmsnorm_f32(x, g_ref[...]).astype(BF16)
    qf = jnp.dot(hx, wq_ref[...], preferred_element_type=F32)
    qf = (qf * (hd ** -0.5 * LOG2E)).astype(BF16)
    scores = []
    for h in range(N_XATTN_HEADS):
        kh = kv_ref[:, h * hd:(h + 1) * hd]
        scores.append(lax.dot_general(qf[:, h * hd:(h + 1) * hd], kh,
                                      (((1,), (1,)), ((), ())),
                                      preferred_element_type=F32))
    for h in range(N_XATTN_HEADS):
        s = scores[h]
        vh = kv_ref[:, d_model + h * hd:d_model + (h + 1) * hd]
        p = jnp.exp2(s - jnp.max(s, axis=1, keepdims=True))
        l = jnp.sum(p, axis=1, keepdims=True)
        oh = jnp.dot(p.astype(BF16), vh, preferred_element_type=F32) / l
        oh_ref[:, h * hd:(h + 1) * hd] = oh.astype(oh_ref.dtype)
    o_ref[...] = x + jnp.dot(oh_ref[...], wo_ref[...], preferred_element_type=F32)


def _cross_attention(x, g, wq, kv, wo):
    b, s, d_model = x.shape
    tm = XATT_TM
    assert s % tm == 0
    return pl.pallas_call(
        _cross_attn_kernel,
        grid=(b, s // tm),
        in_specs=[
            pl.BlockSpec((None, tm, d_model), lambda bi, i: (bi, i, 0)),
            _resident((1, d_model)),
            _resident(wq.shape),
            pl.BlockSpec((None,) + kv.shape[1:], lambda bi, i: (bi, 0, 0)),
            _resident(wo.shape),
        ],
        out_specs=pl.BlockSpec((None, tm, d_model), lambda bi, i: (bi, i, 0)),
        out_shape=jax.ShapeDtypeStruct((b, s, d_model), F32),
        scratch_shapes=[pltpu.VMEM((tm, d_model), BF16)],
        compiler_params=_params(2),
        name="cross_attention",
    )(x, g, wq, kv, wo)


def _ffn_kernel(x_ref, g_ref, wg_ref, wu_ref, wd_ref, gf_ref, o_ref, h_ref, *,
                final_norm):
    j = pl.program_id(1)
    last = pl.num_programs(1) - 1

    def glu_down(h):
        gate = jnp.dot(h, wg_ref[...], preferred_element_type=F32)
        up = jnp.dot(h, wu_ref[...], preferred_element_type=F32)
        act = (gate / (1.0 + jnp.exp(-gate)) * up).astype(BF16)
        return jnp.dot(act, wd_ref[...], preferred_element_type=F32)

    @pl.when(j == 0)
    def _():
        x = x_ref[...]
        h_ref[...] = _rmsnorm_f32(x, g_ref[...]).astype(h_ref.dtype)
        o_ref[...] = x + glu_down(h_ref[...])

    @pl.when(jnp.logical_and(j > 0, j < last))
    def _():
        o_ref[...] += glu_down(h_ref[...])

    @pl.when(j == last)
    def _():
        y = o_ref[...] + glu_down(h_ref[...])
        o_ref[...] = _rmsnorm_f32(y, gf_ref[...]) if final_norm else y


def _ffn(x, g, w_gate, w_up, w_down, g_final, *, final_norm):
    m, d_model = x.shape
    f = w_gate.shape[1]
    tm, tf = FFN_TM, FFN_TF
    assert m % tm == 0 and f % tf == 0 and f // tf >= 2
    return pl.pallas_call(
        functools.partial(_ffn_kernel, final_norm=final_norm),
        grid=(m // tm, f // tf),
        in_specs=[
            pl.BlockSpec((tm, d_model), lambda i, j: (i, 0)),
            pl.BlockSpec((1, d_model), lambda i, j: (0, 0)),
            pl.BlockSpec((d_model, tf), lambda i, j: (0, j)),
            pl.BlockSpec((d_model, tf), lambda i, j: (0, j)),
            pl.BlockSpec((tf, d_model), lambda i, j: (j, 0)),
            pl.BlockSpec((1, d_model), lambda i, j: (0, 0)),
        ],
        out_specs=pl.BlockSpec((tm, d_model), lambda i, j: (i, 0)),
        out_shape=jax.ShapeDtypeStruct((m, d_model), F32),
        scratch_shapes=[pltpu.VMEM((tm, d_model), BF16)],
        compiler_params=_params(2),
        name="swiglu_final_norm",
    )(x, g, w_gate, w_up, w_down, g_final)


def kernel(x, mem, norm_mix, w_in, lambda_q1, lambda_k1, lambda_q2, lambda_k2, subln, pool_w, pool_scale, w_o, norm_xattn, norm_mem, wq_x, wkv_x, wo_x, norm_ffn, w_gate, w_up, w_down, norm_final):
    b, s, d_model = x.shape
    depth = norm_mix.shape[0]
    d_attn = d_model // 2
    n_diff_heads = d_attn // (2 * DIFF_HEAD_DIM)
    d_in = w_in.shape[2]
    t = b * s

    row = lambda v: v.reshape(1, -1).astype(F32)

    for l in range(depth):
        lam_init = _lambda_init(l)
        proj = _norm_matmul(x.reshape(t, d_model), row(norm_mix[l]),
                            w_in[l].astype(BF16), tm=MM_TM, tn=MM_TN, name="in_proj",
                            scale=DIFF_HEAD_DIM ** -0.5 * LOG2E, n_scaled=d_attn)
        proj = proj.reshape(b, s, d_in)
        later = (w_o, wkv_x, wq_x, wo_x, w_gate, w_up, w_down)
        o_attn, (w_o16, wkv16, wq16, wo16, wg16, wu16, wd16) = _diff_attention(
            proj, row(lambda_q1[l]), row(lambda_k1[l]), row(lambda_q2[l]),
            row(lambda_k2[l]), row(subln[l]), later, l,
            n_heads=n_diff_heads, lam_init=lam_init)
        pool_w16 = pool_w[l].astype(BF16)
        x = _pool_out_proj(proj, o_attn, x, pool_w16, row(pool_scale[l]), w_o16)
        kv = _norm_matmul(mem.reshape(b * MEM_LEN, d_model), row(norm_mem[l]),
                          wkv16, tm=b * MEM_LEN, tn=MM_TN, name="mem_kv_proj")
        x = _cross_attention(x, row(norm_xattn[l]), wq16,
                             kv.reshape(b, MEM_LEN, 2 * d_model), wo16)
        x = _ffn(x.reshape(t, d_model), row(norm_ffn[l]), wg16, wu16, wd16,
                 row(norm_final), final_norm=(l == depth - 1)).reshape(b, s, d_model)
    return x
```
